```python
import jax, jax.numpy as jnp
from jax import lax
import numpy as np

D_MODEL = 2048
BATCH = 4
SEQ = 4096
DEPTH = 4

N_MIXERS = 3
CHUNK = 128
EPS = 1e-6

RET_HEADS = 8
RET_DK = D_MODEL // RET_HEADS
RET_DV = 2 * RET_DK
ROPE_BASE = 10000.0

RWKV_HEAD = 64
RWKV_HEADS = D_MODEL // RWKV_HEAD
RWKV_DECAY_LORA = 96
RWKV_A_LORA = 96
RWKV_GATE_LORA = 256
RWKV_IN = 3 * D_MODEL + RWKV_DECAY_LORA + RWKV_A_LORA + RWKV_GATE_LORA
RWKV_LN_EPS = 64e-5

MLSTM_HEADS = 4
MLSTM_DQK = D_MODEL // (2 * MLSTM_HEADS)
MLSTM_DV = D_MODEL // MLSTM_HEADS
MLSTM_CONV = 4
MLSTM_IN = 2 * MLSTM_HEADS * MLSTM_DQK + 2 * D_MODEL + 2 * MLSTM_HEADS
GATE_CAP = 15.0

D_FF = 5632
FFN_CONV = 3

N_RET = (DEPTH + 2) // 3
N_RWKV = (DEPTH + 1) // 3
N_MLSTM = DEPTH // 3

kernel_name = 'hybrid_retention_rwkv7_mlstm_trunk'


def rms_norm(x, g):
    x32 = x.astype(jnp.float32)
    y = x32 * lax.rsqrt(jnp.mean(x32 * x32, axis=-1, keepdims=True) + EPS)
    return (y * g.astype(jnp.float32)).astype(x.dtype)


def head_norm(x, eps, center):
    x = x.astype(jnp.float32)
    if center:
        x = x - jnp.mean(x, axis=-1, keepdims=True)
    return x * lax.rsqrt(jnp.mean(x * x, axis=-1, keepdims=True) + eps)


def modulate(h, shift, scale):
    return h * (1.0 + scale[:, None, :]) + shift[:, None, :]


def rotary(x, pos):
    half = x.shape[-1] // 2
    inv = ROPE_BASE ** (-jnp.arange(half, dtype=jnp.float32) / half)
    ang = pos.astype(jnp.float32)[:, None] * inv[None, :]
    cos = jnp.cos(ang)[None, :, None, :]
    sin = jnp.sin(ang)[None, :, None, :]
    x1, x2 = x[..., :half], x[..., half:]
    return jnp.concatenate([x1 * cos - x2 * sin, x1 * sin + x2 * cos], axis=-1)


def causal_dwconv(x, w, b):
    k_width, ch = w.shape
    y = lax.conv_general_dilated(x, w.reshape(k_width, 1, ch).astype(x.dtype), window_strides=(1,), padding=[(k_width - 1, 0)], dimension_numbers=('NWC', 'WIO', 'NWC'), feature_group_count=ch)
    return y + b


def token_shift(z):
    return jnp.pad(z[:, :-1], ((0, 0), (1, 0), (0, 0)))


def to_chunks(z):
    b, t = z.shape[:2]
    z = z.reshape((b, t // CHUNK, CHUNK) + z.shape[2:])
    return jnp.moveaxis(jnp.moveaxis(z, 1, 0), 2, 3)


def from_chunks(z):
    z = jnp.moveaxis(jnp.moveaxis(z, 3, 2), 0, 1)
    return z.reshape((z.shape[0], z.shape[1] * z.shape[2]) + z.shape[3:])


def retention_chunked(q, k, v, log_gamma):
    n_b, _, n_h, d_k = q.shape
    idx = jnp.arange(CHUNK, dtype=jnp.float32)
    diff = idx[:, None] - idx[None, :]
    causal = diff >= 0
    decay_intra = jnp.where(causal, jnp.exp(log_gamma[:, None, None] * jnp.where(causal, diff, 0.0)), 0.0)
    decay_q = jnp.exp(log_gamma[:, None] * (idx + 1.0))[..., None]
    decay_k = jnp.exp(log_gamma[:, None] * (CHUNK - 1.0 - idx))[..., None]
    decay_chunk = jnp.exp(log_gamma * CHUNK)[:, None, None]

    def step(state, inp):
        qc, kc, vc = inp
        scores = jnp.einsum('bhtd,bhsd->bhts', qc, kc) * decay_intra
        out = jnp.einsum('bhts,bhse->bhte', scores, vc) + jnp.einsum('bhtd,bhde->bhte', qc * decay_q, state)
        state = decay_chunk * state + jnp.einsum('bhsd,bhse->bhde', kc * decay_k, vc)
        return state, out

    s0 = jnp.zeros((n_b, n_h, d_k, v.shape[-1]), jnp.float32)
    _, out = lax.scan(step, s0, (to_chunks(q), to_chunks(k), to_chunks(v)))
    return from_chunks(out)


def mlstm_chunked(q, k, v, logi, logf):
    n_b, _, n_h, d_qk = q.shape
    d_v = v.shape[-1]
    tril = jnp.tril(jnp.ones((CHUNK, CHUNK), dtype=bool))

    def step(carry, inp):
        c_st, n_st, m_st = carry
        qc, kc, vc, li, lf = inp
        b = jnp.cumsum(lf, axis=-1)
        dmat = jnp.where(tril, b[..., :, None] - b[..., None, :] + li[..., None, :], -jnp.inf)
        m_inter = b + m_st[..., None]
        m_t = jnp.maximum(m_inter, jnp.max(dmat, axis=-1))
        scores = jnp.einsum('bhtd,bhsd->bhts', qc, kc) * jnp.exp(dmat - m_t[..., None])
        w_inter = jnp.exp(m_inter - m_t)
        num = jnp.einsum('bhts,bhse->bhte', scores, vc) + w_inter[..., None] * jnp.einsum('bhtd,bhed->bhte', qc, c_st)
        den = jnp.sum(scores, axis=-1) + w_inter * jnp.einsum('bhtd,bhd->bht', qc, n_st)
        h = num / jnp.maximum(jnp.abs(den), jnp.exp(-m_t))[..., None]
        b_last = b[..., -1]
        g = b_last[..., None] - b + li
        m_new = jnp.maximum(b_last + m_st, jnp.max(g, axis=-1))
        wk = jnp.exp(g - m_new[..., None])
        carry_scale = jnp.exp(b_last + m_st - m_new)
        c_st = carry_scale[..., None, None] * c_st + jnp.einsum('bhse,bhsd->bhed', vc * wk[..., None], kc)
        n_st = carry_scale[..., None] * n_st + jnp.einsum('bhs,bhsd->bhd', wk, kc)
        return (c_st, n_st, m_new), h

    init = (jnp.zeros((n_b, n_h, d_v, d_qk), jnp.float32), jnp.zeros((n_b, n_h, d_qk), jnp.float32), jnp.zeros((n_b, n_h), jnp.float32))
    _, h = lax.scan(step, init, (to_chunks(q), to_chunks(k), to_chunks(v), to_chunks(logi), to_chunks(logf)))
    return from_chunks(h)


def rwkv7_scan(r, w, k, v, a, b):
    n_b, _, n_h, n = r.shape

    def step(state, inp):
        r_t, w_t, k_t, v_t, a_t, b_t = inp
        sa = jnp.einsum('bhvk,bhk->bhv', state, a_t)
        state = state * w_t[:, :, None, :] + sa[..., :, None] * b_t[..., None, :] + v_t[..., :, None] * k_t[..., None, :]
        return state, jnp.einsum('bhvk,bhk->bhv', state, r_t)

    xs = tuple(jnp.moveaxis(z, 1, 0) for z in (r, w, k, v, a, b))
    _, y = lax.scan(step, jnp.zeros((n_b, n_h, n, n), jnp.float32), xs)
    return jnp.moveaxis(y, 0, 1)


def retention_mixer(h, w_in, w_out):
    n_b, t_len, _ = h.shape
    qk_w = RET_HEADS * RET_DK
    q, k, v, g = jnp.split(h @ w_in, [qk_w, 2 * qk_w, 2 * qk_w + RET_HEADS * RET_DV], axis=-1)
    pos = jnp.arange(t_len)
    q = rotary(q.reshape(n_b, t_len, RET_HEADS, RET_DK).astype(jnp.float32), pos)
    k = rotary(k.reshape(n_b, t_len, RET_HEADS, RET_DK).astype(jnp.float32), pos) * (RET_DK ** -0.5)
    v = v.reshape(n_b, t_len, RET_HEADS, RET_DV).astype(jnp.float32)
    log_gamma = jnp.log1p(-jnp.power(2.0, -5.0 - jnp.arange(RET_HEADS, dtype=jnp.float32)))
    o = head_norm(retention_chunked(q, k, v, log_gamma), EPS, True)
    o = o.reshape(n_b, t_len, RET_HEADS * RET_DV).astype(h.dtype) * jax.nn.silu(g)
    return o @ w_out


def rwkv7_mixer(h, w_in, mu, w0, w2, a0, a2, g2, k_k, k_a, r_k, ln_w, ln_b, w_out):
    n_b, t_len, d = h.shape
    p = h @ w_in
    p = p + mu * (token_shift(p) - p)
    r, k, v, w_lo, a_lo, g_lo = jnp.split(p, [d, 2 * d, 3 * d, 3 * d + RWKV_DECAY_LORA, 3 * d + RWKV_DECAY_LORA + RWKV_A_LORA], axis=-1)
    w = -jax.nn.softplus(-(w0 + jnp.tanh(w_lo) @ w2)) - 0.5
    decay = jnp.exp(-jnp.exp(w.astype(jnp.float32)))
    a = jax.nn.sigmoid(a0 + a_lo @ a2)
    g = jax.nn.sigmoid(g_lo) @ g2

    def heads(z):
        return z.reshape(n_b, t_len, RWKV_HEADS, RWKV_HEAD).astype(jnp.float32)

    kk = heads(k * k_k)
    kk = kk / jnp.maximum(jnp.sqrt(jnp.sum(kk * kk, axis=-1, keepdims=True)), 1e-12)
    k = k * (1.0 + (a - 1.0) * k_a)
    r_h, k_h, v_h, a_h = heads(r), heads(k), heads(v), heads(a)
    y = rwkv7_scan(r_h, heads(decay), k_h, v_h, -kk, kk * a_h)
    y = head_norm(y, RWKV_LN_EPS, True) * ln_w.reshape(RWKV_HEADS, RWKV_HEAD) + ln_b.reshape(RWKV_HEADS, RWKV_HEAD)
    y = y + jnp.sum(r_h * k_h * r_k, axis=-1, keepdims=True) * v_h
    y = y.reshape(n_b, t_len, d).astype(h.dtype) * g
    return y @ w_out


def softcap(z):
    return GATE_CAP * jnp.tanh(z / GATE_CAP)


def mlstm_mixer(h, w_in, conv_w, conv_b, gate_b, norm_g, w_out):
    n_b, t_len, d = h.shape
    qk_w = 2 * MLSTM_HEADS * MLSTM_DQK
    qk, v, o, gates = jnp.split(h @ w_in, [qk_w, qk_w + d, qk_w + 2 * d], axis=-1)
    qk = jax.nn.silu(causal_dwconv(qk, conv_w, conv_b))
    q, k = jnp.split(qk, 2, axis=-1)
    q = q.reshape(n_b, t_len, MLSTM_HEADS, MLSTM_DQK).astype(jnp.float32)
    k = k.reshape(n_b, t_len, MLSTM_HEADS, MLSTM_DQK).astype(jnp.float32) * (MLSTM_DQK ** -0.5)
    v = v.reshape(n_b, t_len, MLSTM_HEADS, MLSTM_DV).astype(jnp.float32)
    i_pre, f_pre = jnp.split(gates.astype(jnp.float32) + gate_b.astype(jnp.float32), 2, axis=-1)
    logi = softcap(i_pre)
    logf = jax.nn.log_sigmoid(softcap(f_pre))
    hh = head_norm(mlstm_chunked(q, k, v, logi, logf), EPS, False) * norm_g.astype(jnp.float32).reshape(MLSTM_HEADS, MLSTM_DV)
    hh = hh.reshape(n_b, t_len, d).astype(h.dtype) * jax.nn.sigmoid(o)
    return hh @ w_out


def conv_glu_ffn(h, w_up, conv_w, conv_b, w_down):
    gate, val = jnp.split(h @ w_up, 2, axis=-1)
    gate = causal_dwconv(gate, conv_w, conv_b)
    return (jax.nn.silu(gate) * val) @ w_down


def setup_inputs(seed: int = 0) -> dict:
    key = jax.random.key(seed)
    ks = iter(jax.random.split(key, 48))
    D = D_MODEL

    def nrm(shape, scale):
        return scale * jax.random.normal(next(ks), shape, jnp.float32)

    def unif(shape, lo, hi):
        return jax.random.uniform(next(ks), shape, jnp.float32, lo, hi)

    f_bias = jnp.broadcast_to(jnp.linspace(3.0, 6.0, MLSTM_HEADS, dtype=jnp.float32), (N_MLSTM, MLSTM_HEADS))
    gate_b = jnp.concatenate([nrm((N_MLSTM, MLSTM_HEADS), 0.1), f_bias + nrm((N_MLSTM, MLSTM_HEADS), 0.1)], axis=-1)
    return {
        'x': nrm((BATCH, SEQ, D), 1.0),
        'c': nrm((BATCH, D), 1.0),
        'mod_w': nrm((DEPTH, D, 6 * D), 0.5 * D ** -0.5),
        'mod_b': nrm((DEPTH, 6 * D), 0.02),
        'norm_mix_g': 1.0 + nrm((DEPTH, D), 0.02),
        'norm_ffn_g': 1.0 + nrm((DEPTH, D), 0.02),
        'ret_w_in': nrm((N_RET, D, 2 * RET_HEADS * RET_DK + 2 * RET_HEADS * RET_DV), D ** -0.5),
        'ret_w_out': nrm((N_RET, RET_HEADS * RET_DV, D), (RET_HEADS * RET_DV) ** -0.5),
        'rwkv_w_in': nrm((N_RWKV, D, RWKV_IN), D ** -0.5),
        'rwkv_mu': unif((N_RWKV, RWKV_IN), 0.0, 1.0),
        'rwkv_w0': unif((N_RWKV, D), -6.0, -1.0),
        'rwkv_w2': nrm((N_RWKV, RWKV_DECAY_LORA, D), 0.5 * RWKV_DECAY_LORA ** -0.5),
        'rwkv_a0': nrm((N_RWKV, D), 0.1),
        'rwkv_a2': nrm((N_RWKV, RWKV_A_LORA, D), 0.5 * RWKV_A_LORA ** -0.5),
        'rwkv_g2': nrm((N_RWKV, RWKV_GATE_LORA, D), RWKV_GATE_LORA ** -0.5),
        'rwkv_k_k': 0.85 + nrm((N_RWKV, D), 0.05),
        'rwkv_k_a': 1.0 + nrm((N_RWKV, D), 0.05),
        'rwkv_r_k': nrm((N_RWKV, RWKV_HEADS, RWKV_HEAD), 0.1),
        'rwkv_ln_w': 1.0 + nrm((N_RWKV, D), 0.02),
        'rwkv_ln_b': nrm((N_RWKV, D), 0.02),
        'rwkv_w_out': nrm((N_RWKV, D, D), D ** -0.5),
        'mlstm_w_in': nrm((N_MLSTM, D, MLSTM_IN), D ** -0.5),
        'mlstm_conv_w': nrm((N_MLSTM, MLSTM_CONV, 2 * MLSTM_HEADS * MLSTM_DQK), MLSTM_CONV ** -0.5),
        'mlstm_conv_b': nrm((N_MLSTM, 2 * MLSTM_HEADS * MLSTM_DQK), 0.02),
        'mlstm_gate_b': gate_b,
        'mlstm_norm_g': 1.0 + nrm((N_MLSTM, D), 0.02),
        'mlstm_w_out': nrm((N_MLSTM, D, D), D ** -0.5),
        'ffn_w_up': nrm((DEPTH, D, 2 * D_FF), D ** -0.5),
        'ffn_conv_w': nrm((DEPTH, FFN_CONV, D_FF), FFN_CONV ** -0.5),
        'ffn_conv_b': nrm((DEPTH, D_FF), 0.02),
        'ffn_w_down': nrm((DEPTH, D_FF, D), D_FF ** -0.5),
        'final_g': 1.0 + nrm((D,), 0.02),
        'final_mod_w': nrm((D, 2 * D), 0.5 * D ** -0.5),
        'final_mod_b': nrm((2 * D,), 0.02),
    }


def reference(x, c, mod_w, mod_b, norm_mix_g, norm_ffn_g, ret_w_in, ret_w_out, rwkv_w_in, rwkv_mu, rwkv_w0, rwkv_w2, rwkv_a0, rwkv_a2, rwkv_g2, rwkv_k_k, rwkv_k_a, rwkv_r_k, rwkv_ln_w, rwkv_ln_b, rwkv_w_out, mlstm_w_in, mlstm_conv_w, mlstm_conv_b, mlstm_gate_b, mlstm_norm_g, mlstm_w_out, ffn_w_up, ffn_conv_w, ffn_conv_b, ffn_w_down, final_g, final_mod_w, final_mod_b):
    c_act = jax.nn.silu(c)
    for i in range(DEPTH):
        mod = c_act @ mod_w[i] + mod_b[i]
        sh_m, sc_m, g_m, sh_f, sc_f, g_f = jnp.split(mod, 6, axis=-1)
        h = modulate(rms_norm(x, norm_mix_g[i]), sh_m, sc_m)
        kind, j = i % N_MIXERS, i // N_MIXERS
        if kind == 0:
            y = retention_mixer(h, ret_w_in[j], ret_w_out[j])
        elif kind == 1:
            y = rwkv7_mixer(h, rwkv_w_in[j], rwkv_mu[j], rwkv_w0[j], rwkv_w2[j], rwkv_a0[j], rwkv_a2[j], rwkv_g2[j], rwkv_k_k[j], rwkv_k_a[j], rwkv_r_k[j], rwkv_ln_w[j], rwkv_ln_b[j], rwkv_w_out[j])
        else:
            y = mlstm_mixer(h, mlstm_w_in[j], mlstm_conv_w[j], mlstm_conv_b[j], mlstm_gate_b[j], mlstm_norm_g[j], mlstm_w_out[j])
        x = x + g_m[:, None, :] * y
        h = modulate(rms_norm(x, norm_ffn_g[i]), sh_f, sc_f)
        x = x + g_f[:, None, :] * conv_glu_ffn(h, ffn_w_up[i], ffn_conv_w[i], ffn_conv_b[i], ffn_w_down[i])
    sh, sc = jnp.split(c_act @ final_mod_w + final_mod_b, 2, axis=-1)
    return modulate(rms_norm(x, final_g), sh, sc)
```

```python
import functools

import numpy as np
import jax
import jax.numpy as jnp
from jax import lax
from jax.experimental import pallas as pl
from jax.experimental.pallas import tpu as pltpu

F32 = jnp.float32
BF16 = jnp.bfloat16

V7X_VMEM_BYTES = 64 * 1024 * 1024
VMEM_LIMIT = (V7X_VMEM_BYTES * 7) // 8
LANES = 128
SUBLANES = 8
BF16_ROWS = 16

EPS = 1e-6
N_MIXERS = 3

RET_HEADS = 8
ROPE_BASE = 10000.0
RET_CHUNK = 128

RWKV_HEAD = 64
RWKV_DECAY_LORA = 96
RWKV_A_LORA = 96
RWKV_GATE_LORA = 256
RWKV_LN_EPS = 64e-5
RWKV_CHUNK = 64
RWKV_PAIR = 2 * RWKV_HEAD

MLSTM_HEADS = 4
MLSTM_CONV = 4
MLSTM_CHUNK = 128
GATE_CAP = 15.0

FFN_CONV = 3
NEG_BIG = -1e30


def _cparams(*sem):
    return pltpu.CompilerParams(dimension_semantics=sem, vmem_limit_bytes=VMEM_LIMIT)


def _mm(a, b):
    return jnp.dot(a.astype(BF16), b.astype(BF16), preferred_element_type=F32)


def _mm_nt(a, b):
    return lax.dot_general(a.astype(BF16), b.astype(BF16), (((1,), (1,)), ((), ())),
                           preferred_element_type=F32)


def _split(x):
    hi = x.astype(BF16)
    lo = (x - hi.astype(F32)).astype(BF16)
    return hi, lo


def _mm_split_lhs(x, m):
    hi, lo = _split(x)
    return (jnp.dot(hi, m, preferred_element_type=F32)
            + jnp.dot(lo, m, preferred_element_type=F32))


def _mm_split_rhs(m, x):
    hi, lo = _split(x)
    return (jnp.dot(m, hi, preferred_element_type=F32)
            + jnp.dot(m, lo, preferred_element_type=F32))


def _silu(x):
    return x * jax.nn.sigmoid(x)


def _softplus(z):
    return jnp.maximum(z, 0.0) + jnp.log(1.0 + jnp.exp(-jnp.abs(z)))


def _mod_kernel(c_ref, w_ref, b_ref, o_ref):
    ca = _silu(c_ref[...])
    o_ref[0] = jnp.dot(ca, w_ref[0], precision=lax.Precision.HIGHEST,
                       preferred_element_type=F32) + b_ref[0]


def _mod_vectors(c_pad, w, b, tn=1024):
    depth, d, n = w.shape
    rows = c_pad.shape[0]
    return pl.pallas_call(
        _mod_kernel,
        grid=(depth, n // tn),
        in_specs=[pl.BlockSpec((rows, d), lambda i, j: (0, 0)),
                  pl.BlockSpec((1, d, tn), lambda i, j: (i, 0, j)),
                  pl.BlockSpec((1, 1, tn), lambda i, j: (i, 0, j))],
        out_specs=pl.BlockSpec((1, rows, tn), lambda i, j: (i, 0, j)),
        out_shape=jax.ShapeDtypeStruct((depth, rows, n), F32),
        compiler_params=_cparams("parallel", "parallel"),
        name="mod_vectors",
    )(c_pad, w, b.reshape(depth, 1, n))


def _nm_linear_kernel(x_ref, g_ref, sh_ref, sc_ref, w_ref, o_ref, h_ref, *, tm, rchunk):
    col_tile = pl.program_id(1)

    @pl.when(col_tile == 0)
    def _():
        for r0 in range(0, tm, rchunk):
            x = x_ref[r0:r0 + rchunk, :]
            y = x * lax.rsqrt(jnp.mean(x * x, axis=-1, keepdims=True) + EPS)
            h = (y * g_ref[...]) * (1.0 + sc_ref[0]) + sh_ref[0]
            h_ref[r0:r0 + rchunk, :] = h.astype(BF16)

    o_ref[...] = jnp.dot(h_ref[...], w_ref[...],
                         preferred_element_type=F32).astype(o_ref.dtype)


def _nm_linear(x2, g, shift, scale, w, t_len, out_dtype, tm=1024, tn=512):
    n, d = x2.shape
    m = w.shape[1]
    tm = min(tm, t_len)
    tpb = t_len // tm
    kern = functools.partial(_nm_linear_kernel, tm=tm, rchunk=min(256, tm))
    return pl.pallas_call(
        kern,
        grid=(n // tm, m // tn),
        in_specs=[pl.BlockSpec((tm, d), lambda i, j: (i, 0)),
                  pl.BlockSpec((1, d), lambda i, j: (0, 0)),
                  pl.BlockSpec((1, 1, d), lambda i, j: (i // tpb, 0, 0)),
                  pl.BlockSpec((1, 1, d), lambda i, j: (i // tpb, 0, 0)),
                  pl.BlockSpec((d, tn), lambda i, j: (0, j))],
        out_specs=pl.BlockSpec((tm, tn), lambda i, j: (i, j)),
        out_shape=jax.ShapeDtypeStruct((n, m), out_dtype),
        scratch_shapes=[pltpu.VMEM((tm, d), BF16)],
        compiler_params=_cparams("parallel", "arbitrary"),
        name="norm_mod_linear",
    )(x2, g.reshape(1, d), shift, scale, w)


def _linear_res_kernel(a_ref, w_ref, x_ref, gate_ref, o_ref):
    acc = jnp.dot(a_ref[...], w_ref[...], preferred_element_type=F32)
    o_ref[...] = x_ref[...] + gate_ref[0] * acc


def _linear_res(a, w, x2, gate, t_len, tm=512, tn=512):
    n, k = a.shape
    m = w.shape[1]
    tm = min(tm, t_len)
    tpb = t_len // tm
    return pl.pallas_call(
        _linear_res_kernel,
        grid=(n // tm, m // tn),
        in_specs=[pl.BlockSpec((tm, k), lambda i, j: (i, 0)),
                  pl.BlockSpec((k, tn), lambda i, j: (0, j)),
                  pl.BlockSpec((tm, tn), lambda i, j: (i, j)),
                  pl.BlockSpec((1, 1, tn), lambda i, j: (i // tpb, 0, j))],
        out_specs=pl.BlockSpec((tm, tn), lambda i, j: (i, j)),
        out_shape=jax.ShapeDtypeStruct((n, m), F32),
        compiler_params=_cparams("parallel", "arbitrary"),
        name="linear_residual",
    )(a, w, x2, gate)


def _ffn_down_kernel(g_ref, v_ref, prev_ref, cw_ref, cb_ref, w_ref, x_ref, gate_ref,
                     o_ref, act_ref, ext_ref, *, tm, f, cchunk, tpb):
    first = (pl.program_id(0) % tpb) == 0
    col_tile = pl.program_id(1)
    halo = BF16_ROWS

    @pl.when(col_tile == 0)
    def _():
        for c0 in range(0, f, cchunk):
            cs = slice(c0, c0 + cchunk)
            prev = prev_ref[:, cs].astype(F32)
            ext_ref[0:halo, :] = jnp.where(first, 0.0, prev)
            ext_ref[halo:halo + tm, :] = g_ref[:, cs].astype(F32)
            conv = (cw_ref[0:1, cs] * ext_ref[halo - 2:halo - 2 + tm, :]
                    + cw_ref[1:2, cs] * ext_ref[halo - 1:halo - 1 + tm, :]
                    + cw_ref[2:3, cs] * ext_ref[halo:halo + tm, :]
                    + cb_ref[:, cs])
            act = _silu(conv) * v_ref[:, cs].astype(F32)
            act_ref[:, cs] = act.astype(BF16)

    acc = jnp.dot(act_ref[...], w_ref[...], preferred_element_type=F32)
    o_ref[...] = x_ref[...] + gate_ref[0] * acc


def _ffn_down(u, conv_w, conv_b, w, x2, gate, t_len, tm=512, tn=512, cchunk=512):
    n, f2 = u.shape
    f = f2 // 2
    m = w.shape[1]
    tm = min(tm, t_len)
    tpb = t_len // tm
    rb = tm // BF16_ROWS
    kern = functools.partial(_ffn_down_kernel, tm=tm, f=f, cchunk=cchunk, tpb=tpb)
    return pl.pallas_call(
        kern,
        grid=(n // tm, m // tn),
        in_specs=[pl.BlockSpec((tm, f), lambda i, j: (i, 0)),
                  pl.BlockSpec((tm, f), lambda i, j: (i, 1)),
                  pl.BlockSpec((BF16_ROWS, f), lambda i, j: (jnp.maximum(i * rb - 1, 0), 0)),
                  pl.BlockSpec((FFN_CONV, f), lambda i, j: (0, 0)),
                  pl.BlockSpec((1, f), lambda i, j: (0, 0)),
                  pl.BlockSpec((f, tn), lambda i, j: (0, j)),
                  pl.BlockSpec((tm, tn), lambda i, j: (i, j)),
                  pl.BlockSpec((1, 1, tn), lambda i, j: (i // tpb, 0, j))],
        out_specs=pl.BlockSpec((tm, tn), lambda i, j: (i, j)),
        out_shape=jax.ShapeDtypeStruct((n, m), F32),
        scratch_shapes=[pltpu.VMEM((tm, f), BF16),
                        pltpu.VMEM((tm + BF16_ROWS, cchunk), F32)],
        compiler_params=_cparams("parallel", "arbitrary"),
        name="ffn_down",
    )(u, u, u, conv_w, conv_b.reshape(1, f), w, x2, gate)


def _retention_kernel(q_ref, k_ref, v_ref, g_ref, cos_ref, sin_ref, dint_ref, dq_ref, dk_ref,
                      dc_ref, o_ref, s_ref, *, dk):
    chunk = pl.program_id(2)

    @pl.when(chunk == 0)
    def _():
        s_ref[...] = jnp.zeros_like(s_ref)

    cos = cos_ref[...]
    sin = sin_ref[...]
    half = dk // 2

    def rot(x):
        x1, x2 = x[:, :half], x[:, half:]
        return jnp.concatenate([x1 * cos - x2 * sin, x1 * sin + x2 * cos], axis=-1)

    q = rot(q_ref[...])
    k = rot(k_ref[...]) * (dk ** -0.5)
    v = v_ref[...]
    state = s_ref[...]
    scores = _mm_nt(q, k) * dint_ref[0]
    out = _mm(scores, v) + _mm(q * dq_ref[0], state)
    s_ref[...] = dc_ref[0] * state + _mm((k * dk_ref[0]).T, v)

    oc = out - jnp.mean(out, axis=-1, keepdims=True)
    o = oc * lax.rsqrt(jnp.mean(oc * oc, axis=-1, keepdims=True) + EPS)
    o_ref[...] = (o * _silu(g_ref[...])).astype(o_ref.dtype)


def _retention_tables(t_len, dk):
    half = dk // 2
    inv = ROPE_BASE ** (-np.arange(half, dtype=np.float64) / half)
    ang = np.arange(t_len, dtype=np.float64)[:, None] * inv[None, :]
    lg = np.log1p(-np.power(2.0, -5.0 - np.arange(RET_HEADS, dtype=np.float64)))
    idx = np.arange(RET_CHUNK, dtype=np.float64)
    diff = idx[:, None] - idx[None, :]
    dint = np.where(diff >= 0, np.exp(lg[:, None, None] * np.maximum(diff, 0.0)), 0.0)
    dq = np.exp(lg[:, None] * (idx + 1.0))[..., None]
    dkk = np.exp(lg[:, None] * (RET_CHUNK - 1.0 - idx))[..., None]
    dc = np.exp(lg * RET_CHUNK)[:, None, None]
    f = lambda a: jnp.asarray(a, F32)
    return f(np.cos(ang)), f(np.sin(ang)), f(dint), f(dq), f(dkk), f(dc)


def _retention(qkvg, n_b, t_len, d):
    n = qkvg.shape[0]
    heads = RET_HEADS
    dk = d // heads
    dv = 2 * dk
    lc = RET_CHUNK
    nc = t_len // lc
    cos, sin, dint, dq, dkk, dc = _retention_tables(t_len, dk)
    qb, vb = d // dk, (2 * d) // dv
    gb = vb + heads
    kern = functools.partial(_retention_kernel, dk=dk)
    return pl.pallas_call(
        kern,
        grid=(n_b, heads, nc),
        in_specs=[pl.BlockSpec((lc, dk), lambda b, h, c: (b * nc + c, h)),
                  pl.BlockSpec((lc, dk), lambda b, h, c: (b * nc + c, qb + h)),
                  pl.BlockSpec((lc, dv), lambda b, h, c: (b * nc + c, vb + h)),
                  pl.BlockSpec((lc, dv), lambda b, h, c: (b * nc + c, gb + h)),
                  pl.BlockSpec((lc, dk // 2), lambda b, h, c: (c, 0)),
                  pl.BlockSpec((lc, dk // 2), lambda b, h, c: (c, 0)),
                  pl.BlockSpec((1, lc, lc), lambda b, h, c: (h, 0, 0)),
                  pl.BlockSpec((1, lc, 1), lambda b, h, c: (h, 0, 0)),
                  pl.BlockSpec((1, lc, 1), lambda b, h, c: (h, 0, 0)),
                  pl.BlockSpec((1, 1, 1), lambda b, h, c: (h, 0, 0))],
        out_specs=pl.BlockSpec((lc, dv), lambda b, h, c: (b * nc + c, h)),
        out_shape=jax.ShapeDtypeStruct((n, heads * dv), BF16),
        scratch_shapes=[pltpu.VMEM((dk, dv), F32)],
        compiler_params=_cparams("parallel", "parallel", "arbitrary"),
        name="retention_chunk",
    )(qkvg, qkvg, qkvg, qkvg, cos, sin, dint, dq, dkk, dc)


def _mlstm_kernel(q_ref, k_ref, v_ref, og_ref, gt_ref, cwq_ref, cbq_ref, cwk_ref, cbk_ref,
                  gb_ref, ng_ref, tril_ref, o_ref, qext, kext, c_ref, n_ref, m_ref,
                  *, lc, dqk, heads):
    head = pl.program_id(1)
    chunk = pl.program_id(2)

    @pl.when(chunk == 0)
    def _():
        qext[0:SUBLANES, :] = jnp.zeros((SUBLANES, dqk), F32)
        kext[0:SUBLANES, :] = jnp.zeros((SUBLANES, dqk), F32)
        c_ref[...] = jnp.zeros_like(c_ref)
        n_ref[...] = jnp.zeros_like(n_ref)
        m_ref[...] = jnp.zeros_like(m_ref)

    def conv_silu(raw_ref, ext, cw_ref, cb_ref):
        ext[SUBLANES:SUBLANES + lc, :] = raw_ref[...]
        y = cb_ref[...]
        for j in range(MLSTM_CONV):
            off = SUBLANES - (MLSTM_CONV - 1) + j
            y = y + cw_ref[j:j + 1, :] * ext[off:off + lc, :]
        ext[0:SUBLANES, :] = ext[lc:lc + SUBLANES, :]
        return _silu(y)

    q = conv_silu(q_ref, qext, cwq_ref, cbq_ref)
    k = conv_silu(k_ref, kext, cwk_ref, cbk_ref) * (dqk ** -0.5)
    v = v_ref[...]

    gt = gt_ref[...] + gb_ref[...]
    lane = lax.broadcasted_iota(jnp.int32, gt.shape, 1)
    ipre = jnp.sum(jnp.where(lane == head, gt, 0.0), axis=-1, keepdims=True)
    fpre = jnp.sum(jnp.where(lane == head + heads, gt, 0.0), axis=-1, keepdims=True)
    li = GATE_CAP * jnp.tanh(ipre / GATE_CAP)
    lf = -_softplus(-(GATE_CAP * jnp.tanh(fpre / GATE_CAP)))

    tril = tril_ref[...]
    b_full = _mm_split_rhs(tril, jnp.broadcast_to(lf, (lc, lc)))
    b_col = b_full[:, 0:1]
    b_row = b_full.T
    li_row = jnp.broadcast_to(li, (lc, lc)).T
    causal = (lax.broadcasted_iota(jnp.int32, (lc, lc), 0)
              >= lax.broadcasted_iota(jnp.int32, (lc, lc), 1))
    dmat = jnp.where(causal, b_full - b_row + li_row, NEG_BIG)

    m_st = m_ref[...]
    m_inter = b_col + m_st
    m_t = jnp.maximum(m_inter, jnp.max(dmat, axis=-1, keepdims=True))
    scores = _mm_nt(q, k) * jnp.exp(dmat - m_t)
    w_inter = jnp.exp(m_inter - m_t)
    c_st = c_ref[...]
    n_st = n_ref[...]
    num = _mm(scores, v) + w_inter * _mm(q, c_st)
    den = (jnp.sum(scores, axis=-1, keepdims=True)
           + w_inter * jnp.sum(q * n_st, axis=-1, keepdims=True))
    hh = num / jnp.maximum(jnp.abs(den), jnp.exp(-m_t))

    b_last = b_col[lc - 1:lc, :]
    gdec = b_last - b_col + li
    m_new = jnp.maximum(b_last + m_st, jnp.max(gdec, axis=0, keepdims=True))
    wk = jnp.exp(gdec - m_new)
    carry = jnp.exp(b_last + m_st - m_new)
    kw = k * wk
    c_ref[...] = carry * c_st + _mm(kw.T, v)
    n_ref[...] = carry * n_st + jnp.sum(kw, axis=0, keepdims=True)
    m_ref[...] = m_new

    hn = hh * lax.rsqrt(jnp.mean(hh * hh, axis=-1, keepdims=True) + EPS) * ng_ref[...]
    o_ref[...] = (hn * jax.nn.sigmoid(og_ref[...])).astype(o_ref.dtype)


def _mlstm(proj, conv_w, conv_b, gate_b, norm_g, n_b, t_len, d):
    n = proj.shape[0]
    heads = MLSTM_HEADS
    dqk = d // (2 * heads)
    dv = d // heads
    lc = MLSTM_CHUNK
    nc = t_len // lc
    qk_w = 2 * heads * dqk
    kb = heads
    vb = qk_w // dv
    ob = vb + heads
    gtb = (qk_w + 2 * d) // LANES
    gb = jnp.zeros((1, LANES), F32).at[0, :2 * heads].set(gate_b.astype(F32))
    tril = jnp.asarray(np.tril(np.ones((lc, lc), np.float32)), BF16)
    kern = functools.partial(_mlstm_kernel, lc=lc, dqk=dqk, heads=heads)
    row = lambda b, h, c: b * nc + c
    return pl.pallas_call(
        kern,
        grid=(n_b, heads, nc),
        in_specs=[pl.BlockSpec((lc, dqk), lambda b, h, c: (row(b, h, c), h)),
                  pl.BlockSpec((lc, dqk), lambda b, h, c: (row(b, h, c), kb + h)),
                  pl.BlockSpec((lc, dv), lambda b, h, c: (row(b, h, c), vb + h)),
                  pl.BlockSpec((lc, dv), lambda b, h, c: (row(b, h, c), ob + h)),
                  pl.BlockSpec((lc, LANES), lambda b, h, c: (row(b, h, c), gtb)),
                  pl.BlockSpec((MLSTM_CONV, dqk), lambda b, h, c: (0, h)),
                  pl.BlockSpec((1, dqk), lambda b, h, c: (0, h)),
                  pl.BlockSpec((MLSTM_CONV, dqk), lambda b, h, c: (0, kb + h)),
                  pl.BlockSpec((1, dqk), lambda b, h, c: (0, kb + h)),
                  pl.BlockSpec((1, LANES), lambda b, h, c: (0, 0)),
                  pl.BlockSpec((1, dv), lambda b, h, c: (0, h)),
                  pl.BlockSpec((lc, lc), lambda b, h, c: (0, 0))],
        out_specs=pl.BlockSpec((lc, dv), lambda b, h, c: (row(b, h, c), h)),
        out_shape=jax.ShapeDtypeStruct((n, d), BF16),
        scratch_shapes=[pltpu.VMEM((lc + SUBLANES, dqk), F32),
                        pltpu.VMEM((lc + SUBLANES, dqk), F32),
                        pltpu.VMEM((dqk, dv), F32),
                        pltpu.VMEM((1, dqk), F32),
                        pltpu.VMEM((1, 1), F32)],
        compiler_params=_cparams("parallel", "parallel", "arbitrary"),
        name="mlstm_chunk",
    )(proj, proj, proj, proj, proj, conv_w, conv_b.reshape(1, qk_w), conv_w,
      conv_b.reshape(1, qk_w), gb, norm_g.reshape(1, d), tril)


def _rwkv_prep_kernel(p_ref, prev_ref, mu_ref, w0_ref, a0_ref, kk_ref, ka_ref, w2_ref, a2_ref,
                      g2_ref, ones_ref, r_out, wl_out, k_out, v_out, kn_out, ag_out, g_out,
                      ext, *, tm, d, tpb):
    first = (pl.program_id(0) % tpb) == 0

    def shifted(c0, width):
        outs = []
        for s0 in range(c0, c0 + width, LANES):
            cs = slice(s0, s0 + LANES)
            ext[0:SUBLANES, :] = jnp.where(first, 0.0, prev_ref[:, cs])
            ext[SUBLANES:SUBLANES + tm, :] = p_ref[:, cs]
            cur = p_ref[:, cs]
            outs.append(cur + mu_ref[:, cs] * (ext[SUBLANES - 1:SUBLANES - 1 + tm, :] - cur))
        return outs[0] if len(outs) == 1 else jnp.concatenate(outs, axis=-1)

    wlo = jnp.tanh(shifted(3 * d, LANES)).astype(BF16)
    alo = shifted(3 * d + LANES, LANES).astype(BF16)
    glo = jax.nn.sigmoid(shifted(3 * d + 2 * LANES, RWKV_GATE_LORA)).astype(BF16)
    ones = ones_ref[...]

    for j in range(d // LANES):
        sl = slice(j * LANES, (j + 1) * LANES)
        r = shifted(j * LANES, LANES)
        k = shifted(d + j * LANES, LANES)
        v = shifted(2 * d + j * LANES, LANES)
        wraw = w0_ref[:, sl] + jnp.dot(wlo, w2_ref[:, sl], preferred_element_type=F32)
        wfin = -_softplus(-wraw) - 0.5
        ag = jax.nn.sigmoid(a0_ref[:, sl] + jnp.dot(alo, a2_ref[:, sl],
                                                    preferred_element_type=F32))
        kn = k * kk_ref[:, sl]
        ss = _mm_split_lhs(kn * kn, ones)
        kn = kn / jnp.maximum(jnp.sqrt(ss), 1e-12)
        r_out[0, j] = r
        wl_out[0, j] = -jnp.exp(wfin)
        k_out[0, j] = k * (1.0 + (ag - 1.0) * ka_ref[:, sl])
        v_out[0, j] = v
        kn_out[0, j] = kn
        ag_out[0, j] = ag
        g_out[:, sl] = jnp.dot(glo, g2_ref[:, sl], preferred_element_type=F32)


def _pair_ones(scale=1.0):
    blk = np.kron(np.eye(2, dtype=np.float32), np.ones((RWKV_HEAD, RWKV_HEAD), np.float32))
    return jnp.asarray(blk * scale, BF16)


def _rwkv_prep(p, mu, w0, a0, k_k, k_a, w2, a2, g2, n_b, t_len, d, tm=256):
    n, pw = p.shape
    tm = min(tm, t_len)
    tpb = t_len // tm
    rb = tm // SUBLANES
    npair = d // LANES
    pair_shape = jax.ShapeDtypeStruct((n_b, npair, t_len, LANES), F32)
    pair_spec = pl.BlockSpec((1, npair, tm, LANES), lambda i: (i // tpb, 0, i % tpb, 0))
    vec = lambda a: a.reshape(1, -1)
    cvec = lambda w: pl.BlockSpec((1, w), lambda i: (0, 0))
    kern = functools.partial(_rwkv_prep_kernel, tm=tm, d=d, tpb=tpb)
    return pl.pallas_call(
        kern,
        grid=(n // tm,),
        in_specs=[pl.BlockSpec((tm, pw), lambda i: (i, 0)),
                  pl.BlockSpec((SUBLANES, pw), lambda i: (jnp.maximum(i * rb - 1, 0), 0)),
                  cvec(pw), cvec(d), cvec(d), cvec(d), cvec(d),
                  pl.BlockSpec((LANES, d), lambda i: (0, 0)),
                  pl.BlockSpec((LANES, d), lambda i: (0, 0)),
                  pl.BlockSpec((RWKV_GATE_LORA, d), lambda i: (0, 0)),
                  pl.BlockSpec((LANES, LANES), lambda i: (0, 0))],
        out_specs=[pair_spec] * 6 + [pl.BlockSpec((tm, d), lambda i: (i, 0))],
        out_shape=[pair_shape] * 6 + [jax.ShapeDtypeStruct((n, d), F32)],
        scratch_shapes=[pltpu.VMEM((tm + SUBLANES, LANES), F32)],
        compiler_params=_cparams("parallel"),
        name="rwkv_prep",
    )(p, p, vec(mu), vec(w0), vec(a0), vec(k_k), vec(k_a), w2, a2, g2, _pair_ones())


def _rwkv_scan_kernel(r_ref, wl_ref, k_ref, v_ref, kn_ref, ag_ref, g_ref, lnw_ref, lnb_ref,
                      rk_ref, tril_ref, ms_ref, mi_ref, eye_ref, avg_ref, ones_ref,
                      o_ref, h_ref, y_ref, *, lc, nchunk):
    block = pl.program_id(2)

    @pl.when(block == 0)
    def _():
        h_ref[...] = jnp.zeros_like(h_ref)

    lane = lax.broadcasted_iota(jnp.int32, (lc, LANES), 1)
    head0 = lane < RWKV_HEAD

    def stack(x):
        return jnp.concatenate([jnp.where(head0, x, 0.0), jnp.where(head0, 0.0, x)], axis=0)

    tril = tril_ref[...]
    ms = ms_ref[...]
    mi = mi_ref[...]
    eye = eye_ref[...]
    two = 2 * lc
    hst = h_ref[...]
    for ci in range(nchunk):
        rows = slice(ci * lc, (ci + 1) * lc)
        r = r_ref[0, 0, rows, :]
        wl = wl_ref[0, 0, rows, :]
        k = k_ref[0, 0, rows, :]
        v = v_ref[0, 0, rows, :]
        kn = kn_ref[0, 0, rows, :]
        bb = kn * ag_ref[0, 0, rows, :]

        cum = _mm_split_rhs(tril, wl)
        cl = cum[lc - 1:lc, :]
        e_n = jnp.exp(-cum)
        e_l = jnp.exp(cl - cum)
        a_s = stack(-kn * jnp.exp(cum - wl))
        r_s = stack(r * jnp.exp(cum))
        b_s = stack(bb * e_n)
        k_s = stack(k * e_n)
        v_s = stack(v)
        bp_s = stack(bb * e_l)
        kp_s = stack(k * e_l)

        gm = _mm_nt(jnp.concatenate([a_s, r_s], axis=0), jnp.concatenate([b_s, k_s], axis=0))
        aab = gm[:two, :two] * ms
        aak = gm[:two, two:] * ms
        arb = gm[two:, :two] * mi
        ark = gm[two:, two:] * mi

        npow = aab
        tinv = eye + npow
        steps = int(np.log2(lc)) - 1
        for _ in range(steps):
            npow = _mm(npow, npow)
            tinv = _mm(tinv, eye + npow)

        taw = _mm(tinv, jnp.concatenate([a_s, _mm(aak, v_s)], axis=1))
        arb_taw = _mm(arb, taw)
        qh = r_s + arb_taw[:, :LANES]
        yin = arb_taw[:, LANES:] + _mm(ark, v_s)
        bp_taw = _mm(bp_s.T, taw)
        phi = eye * jnp.exp(cl) + bp_taw[:, :LANES]
        psi = bp_taw[:, LANES:] + _mm(kp_s.T, v_s)

        ys = _mm(qh, hst) + yin
        hst = _mm(phi, hst) + psi
        y_ref[rows, :] = ys[:lc] + ys[lc:]
    h_ref[...] = hst

    y = y_ref[...]
    avg = avg_ref[...]
    yc = y - _mm_split_lhs(y, avg)
    var = _mm_split_lhs(yc * yc, avg)
    yn = yc * lax.rsqrt(var + RWKV_LN_EPS) * lnw_ref[...] + lnb_ref[...]
    rk = _mm_split_lhs(r_ref[0, 0] * k_ref[0, 0] * rk_ref[...], ones_ref[...])
    o_ref[...] = ((yn + rk * v_ref[0, 0]) * g_ref[...]).astype(o_ref.dtype)


def _rwkv_scan(r, wl, k, v, kn, ag, g, ln_w, ln_b, r_k, n_b, t_len, d, nchunk=4):
    lc = RWKV_CHUNK
    lb = lc * nchunk
    nb = t_len // lb
    npair = d // LANES
    two = 2 * lc
    idx = np.arange(two)
    same = (idx[:, None] // lc) == (idx[None, :] // lc)
    ms = jnp.asarray((same & (idx[:, None] > idx[None, :])).astype(np.float32))
    mi = jnp.asarray((same & (idx[:, None] >= idx[None, :])).astype(np.float32))
    eye = jnp.asarray(np.eye(two, dtype=np.float32))
    tril = jnp.asarray(np.tril(np.ones((lc, lc), np.float32)), BF16)
    pair_spec = pl.BlockSpec((1, 1, lb, LANES), lambda b, p, t: (b, p, t, 0))
    vec_spec = pl.BlockSpec((1, LANES), lambda b, p, t: (0, p))
    sq = lambda s: pl.BlockSpec((s, s), lambda b, p, t: (0, 0))
    kern = functools.partial(_rwkv_scan_kernel, lc=lc, nchunk=nchunk)
    return pl.pallas_call(
        kern,
        grid=(n_b, npair, nb),
        in_specs=[pair_spec] * 6
        + [pl.BlockSpec((lb, LANES), lambda b, p, t: (b * nb + t, p)),
           vec_spec, vec_spec, vec_spec, sq(lc), sq(two), sq(two), sq(two), sq(LANES), sq(LANES)],
        out_specs=pl.BlockSpec((lb, LANES), lambda b, p, t: (b * nb + t, p)),
        out_shape=jax.ShapeDtypeStruct((n_b * t_len, d), BF16),
        scratch_shapes=[pltpu.VMEM((LANES, LANES), F32),
                        pltpu.VMEM((lb, LANES), F32)],
        compiler_params=_cparams("parallel", "parallel", "arbitrary"),
        name="rwkv_scan",
    )(r, wl, k, v, kn, ag, g, ln_w.reshape(1, d), ln_b.reshape(1, d), r_k.reshape(1, d),
      tril, ms, mi, eye, _pair_ones(1.0 / RWKV_HEAD), _pair_ones())


def _final_kernel(x_ref, g_ref, sh_ref, sc_ref, o_ref):
    x = x_ref[...]
    y = x * lax.rsqrt(jnp.mean(x * x, axis=-1, keepdims=True) + EPS)
    o_ref[...] = (y * g_ref[...]) * (1.0 + sc_ref[0]) + sh_ref[0]


def _final_norm(x2, g, shift, scale, t_len, tm=256):
    n, d = x2.shape
    tm = min(tm, t_len)
    tpb = t_len // tm
    return pl.pallas_call(
        _final_kernel,
        grid=(n // tm,),
        in_specs=[pl.BlockSpec((tm, d), lambda i: (i, 0)),
                  pl.BlockSpec((1, d), lambda i: (0, 0)),
                  pl.BlockSpec((1, 1, d), lambda i: (i // tpb, 0, 0)),
                  pl.BlockSpec((1, 1, d), lambda i: (i // tpb, 0, 0))],
        out_specs=pl.BlockSpec((tm, d), lambda i: (i, 0)),
        out_shape=jax.ShapeDtypeStruct((n, d), F32),
        compiler_params=_cparams("parallel"),
        name="final_norm",
    )(x2, g.reshape(1, d), shift, scale)


def _pad_cols(w, width):
    return jnp.pad(w, ((0, 0), (0, width - w.shape[1])))


def _rwkv_pack_in(w_in, mu, d):
    c0 = 3 * d
    c1 = c0 + RWKV_DECAY_LORA
    c2 = c1 + RWKV_A_LORA
    def pack(a):
        return jnp.concatenate([a[..., :c0], _pad_cols(a[..., c0:c1], LANES),
                                _pad_cols(a[..., c1:c2], LANES), a[..., c2:]], axis=-1)
    return pack(w_in), pack(mu.reshape(1, -1))[0]


def _pad_rows(w, rows):
    return jnp.pad(w, ((0, rows - w.shape[0]), (0, 0)))


def kernel(x, c, mod_w, mod_b, norm_mix_g, norm_ffn_g, ret_w_in, ret_w_out, rwkv_w_in, rwkv_mu, rwkv_w0, rwkv_w2, rwkv_a0, rwkv_a2, rwkv_g2, rwkv_k_k, rwkv_k_a, rwkv_r_k, rwkv_ln_w, rwkv_ln_b, rwkv_w_out, mlstm_w_in, mlstm_conv_w, mlstm_conv_b, mlstm_gate_b, mlstm_norm_g, mlstm_w_out, ffn_w_up, ffn_conv_w, ffn_conv_b, ffn_w_down, final_g, final_mod_w, final_mod_b):
    n_b, t_len, d = x.shape
    depth = mod_w.shape[0]
    x2 = x.reshape(n_b * t_len, d)

    c_pad = jnp.pad(c, ((0, SUBLANES - n_b), (0, 0)))
    mod = _mod_vectors(c_pad, mod_w, mod_b)[:, :n_b]
    fmod = _mod_vectors(c_pad, final_mod_w[None], final_mod_b[None])[0, :n_b]

    def mvec(i, j):
        return mod[i, :, j * d:(j + 1) * d][:, None, :]

    for i in range(depth):
        sh_m, sc_m, g_m, sh_f, sc_f, g_f = (mvec(i, j) for j in range(6))
        kind, j = i % N_MIXERS, i // N_MIXERS
        if kind == 0:
            qkvg = _nm_linear(x2, norm_mix_g[i], sh_m, sc_m, ret_w_in[j].astype(BF16), t_len, F32)
            o = _retention(qkvg, n_b, t_len, d)
            x2 = _linear_res(o, ret_w_out[j].astype(BF16), x2, g_m, t_len)
        elif kind == 1:
            w_in, mu = _rwkv_pack_in(rwkv_w_in[j], rwkv_mu[j], d)
            p = _nm_linear(x2, norm_mix_g[i], sh_m, sc_m, w_in.astype(BF16), t_len, F32)
            r, wl, k, v, kn, ag, g = _rwkv_prep(
                p, mu, rwkv_w0[j], rwkv_a0[j], rwkv_k_k[j], rwkv_k_a[j],
                _pad_rows(rwkv_w2[j], LANES).astype(BF16), _pad_rows(rwkv_a2[j], LANES).astype(BF16),
                rwkv_g2[j].astype(BF16), n_b, t_len, d)
            o = _rwkv_scan(r, wl, k, v, kn, ag, g, rwkv_ln_w[j], rwkv_ln_b[j], rwkv_r_k[j],
                           n_b, t_len, d)
            x2 = _linear_res(o, rwkv_w_out[j].astype(BF16), x2, g_m, t_len)
        else:
            w_in = mlstm_w_in[j]
            w_in = _pad_cols(w_in, -(-w_in.shape[1] // 512) * 512)
            proj = _nm_linear(x2, norm_mix_g[i], sh_m, sc_m, w_in.astype(BF16), t_len, F32)
            o = _mlstm(proj, mlstm_conv_w[j], mlstm_conv_b[j], mlstm_gate_b[j], mlstm_norm_g[j],
                       n_b, t_len, d)
            x2 = _linear_res(o, mlstm_w_out[j].astype(BF16), x2, g_m, t_len)
        u = _nm_linear(x2, norm_ffn_g[i], sh_f, sc_f, ffn_w_up[i].astype(BF16), t_len, BF16)
        x2 = _ffn_down(u, ffn_conv_w[i], ffn_conv_b[i], ffn_w_down[i].astype(BF16), x2, g_f, t_len)

    out = _final_norm(x2, final_g, fmod[:, :d][:, None, :], fmod[:, d:][:, None, :], t_len)
    return out.reshape(n_b, t_len, d)
```

```python
import functools

import numpy as np
import jax
import jax.numpy as jnp
from jax import lax
from jax.experimental import pallas as pl
from jax.experimental.pallas import tpu as pltpu

F32 = jnp.float32
BF16 = jnp.bfloat16

V7X_VMEM_BYTES = 64 * 1024 * 1024
VMEM_LIMIT = (V7X_VMEM_BYTES * 7) // 8
LANES = 128
SUBLANES = 8
BF16_ROWS = 16

EPS = 1e-6
N_MIXERS = 3

RET_HEADS = 8
ROPE_BASE = 10000.0
RET_CHUNK = 128

RWKV_HEAD = 64
RWKV_DECAY_LORA = 96
RWKV_A_LORA = 96
RWKV_GATE_LORA = 256
RWKV_LN_EPS = 64e-5
RWKV_CHUNK = 64
RWKV_PAIR = 2 * RWKV_HEAD

MLSTM_HEADS = 4
MLSTM_CONV = 4
MLSTM_CHUNK = 128
GATE_CAP = 15.0

FFN_CONV = 3
NEG_BIG = -1e30


def _cparams(*sem):
    return pltpu.CompilerParams(dimension_semantics=sem, vmem_limit_bytes=VMEM_LIMIT)


def _mm(a, b):
    return jnp.dot(a.astype(BF16), b.astype(BF16), preferred_element_type=F32)


def _mm_nt(a, b):
    return lax.dot_general(a.astype(BF16), b.astype(BF16), (((1,), (1,)), ((), ())),
                           preferred_element_type=F32)


def _split(x):
    hi = x.astype(BF16)
    lo = (x - hi.astype(F32)).astype(BF16)
    return hi, lo


def _mm_split_lhs(x, m):
    hi, lo = _split(x)
    return (jnp.dot(hi, m, preferred_element_type=F32)
            + jnp.dot(lo, m, preferred_element_type=F32))


def _mm_split_rhs(m, x):
    hi, lo = _split(x)
    return (jnp.dot(m, hi, preferred_element_type=F32)
            + jnp.dot(m, lo, preferred_element_type=F32))


def _silu(x):
    return x * jax.nn.sigmoid(x)


def _softplus(z):
    return jnp.maximum(z, 0.0) + jnp.log(1.0 + jnp.exp(-jnp.abs(z)))


def _rms_mod(x, g, shift, scale):
    y = x * lax.rsqrt(jnp.mean(x * x, axis=-1, keepdims=True) + EPS)
    return (y * g) * (1.0 + scale) + shift


def _mod_kernel(c_ref, w_ref, b_ref, o_ref):
    ca = _silu(c_ref[...])
    o_ref[0] = jnp.dot(ca, w_ref[0], precision=lax.Precision.HIGHEST,
                       preferred_element_type=F32) + b_ref[0]


def _mod_vectors(c_pad, w, b, tn=1024):
    depth, d, n = w.shape
    rows = c_pad.shape[0]
    return pl.pallas_call(
        _mod_kernel,
        grid=(depth, n // tn),
        in_specs=[pl.BlockSpec((rows, d), lambda i, j: (0, 0)),
                  pl.BlockSpec((1, d, tn), lambda i, j: (i, 0, j)),
                  pl.BlockSpec((1, 1, tn), lambda i, j: (i, 0, j))],
        out_specs=pl.BlockSpec((1, rows, tn), lambda i, j: (i, 0, j)),
        out_shape=jax.ShapeDtypeStruct((depth, rows, n), F32),
        compiler_params=_cparams("parallel", "parallel"),
        name="mod_vectors",
    )(c_pad, w, b.reshape(depth, 1, n))


def _nm_linear_kernel(x_ref, g_ref, sh_ref, sc_ref, w_ref, o_ref, h_ref, *, tm, rchunk):
    col_tile = pl.program_id(1)

    @pl.when(col_tile == 0)
    def _():
        for r0 in range(0, tm, rchunk):
            h = _rms_mod(x_ref[r0:r0 + rchunk, :], g_ref[...], sh_ref[0], sc_ref[0])
            h_ref[r0:r0 + rchunk, :] = h.astype(BF16)

    o_ref[...] = jnp.dot(h_ref[...], w_ref[...],
                         preferred_element_type=F32).astype(o_ref.dtype)


def _nm_linear(x2, g, shift, scale, w, t_len, out_dtype, tm=1024, tn=512):
    n, d = x2.shape
    m = w.shape[1]
    tm = min(tm, t_len)
    tpb = t_len // tm
    kern = functools.partial(_nm_linear_kernel, tm=tm, rchunk=min(256, tm))
    return pl.pallas_call(
        kern,
        grid=(n // tm, m // tn),
        in_specs=[pl.BlockSpec((tm, d), lambda i, j: (i, 0)),
                  pl.BlockSpec((1, d), lambda i, j: (0, 0)),
                  pl.BlockSpec((1, 1, d), lambda i, j: (i // tpb, 0, 0)),
                  pl.BlockSpec((1, 1, d), lambda i, j: (i // tpb, 0, 0)),
                  pl.BlockSpec((d, tn), lambda i, j: (0, j))],
        out_specs=pl.BlockSpec((tm, tn), lambda i, j: (i, j)),
        out_shape=jax.ShapeDtypeStruct((n, m), out_dtype),
        scratch_shapes=[pltpu.VMEM((tm, d), BF16)],
        compiler_params=_cparams("parallel", "arbitrary"),
        name="norm_mod_linear",
    )(x2, g.reshape(1, d), shift, scale, w)


def _ffn_up_kernel(x_ref, xp_ref, g_ref, sh_ref, sc_ref, wg_ref, wv_ref, cw_ref, cb_ref,
                   o_ref, h_ref, hp_ref, ext_ref, *, tm, rchunk, tpb):
    first = (pl.program_id(0) % tpb) == 0
    col_tile = pl.program_id(1)
    halo = BF16_ROWS

    @pl.when(col_tile == 0)
    def _():
        for r0 in range(0, tm, rchunk):
            h = _rms_mod(x_ref[r0:r0 + rchunk, :], g_ref[...], sh_ref[0], sc_ref[0])
            h_ref[r0:r0 + rchunk, :] = h.astype(BF16)
        hp = _rms_mod(xp_ref[...], g_ref[...], sh_ref[0], sc_ref[0])
        hp_ref[...] = hp.astype(BF16)

    wg = wg_ref[...]
    gate_prev = jnp.dot(hp_ref[...], wg, preferred_element_type=F32)
    ext_ref[0:halo, :] = jnp.where(first, 0.0, gate_prev)
    ext_ref[halo:halo + tm, :] = jnp.dot(h_ref[...], wg, preferred_element_type=F32)
    val = jnp.dot(h_ref[...], wv_ref[...], preferred_element_type=F32)
    for r0 in range(0, tm, rchunk):
        base = halo + r0
        conv = (cw_ref[0:1, :] * ext_ref[base - 2:base - 2 + rchunk, :]
                + cw_ref[1:2, :] * ext_ref[base - 1:base - 1 + rchunk, :]
                + cw_ref[2:3, :] * ext_ref[base:base + rchunk, :]
                + cb_ref[...])
        o_ref[r0:r0 + rchunk, :] = (_silu(conv) * val[r0:r0 + rchunk, :]).astype(o_ref.dtype)


def _ffn_up(x2, g, shift, scale, w_up, conv_w, conv_b, t_len, tm=1024, tn=512):
    n, d = x2.shape
    f = w_up.shape[1] // 2
    tm = min(tm, t_len)
    tpb = t_len // tm
    rb = tm // BF16_ROWS
    nf = f // tn
    kern = functools.partial(_ffn_up_kernel, tm=tm, rchunk=min(256, tm), tpb=tpb)
    return pl.pallas_call(
        kern,
        grid=(n // tm, nf),
        in_specs=[pl.BlockSpec((tm, d), lambda i, j: (i, 0)),
                  pl.BlockSpec((BF16_ROWS, d), lambda i, j: (jnp.maximum(i * rb - 1, 0), 0)),
                  pl.BlockSpec((1, d), lambda i, j: (0, 0)),
                  pl.BlockSpec((1, 1, d), lambda i, j: (i // tpb, 0, 0)),
                  pl.BlockSpec((1, 1, d), lambda i, j: (i // tpb, 0, 0)),
                  pl.BlockSpec((d, tn), lambda i, j: (0, j)),
                  pl.BlockSpec((d, tn), lambda i, j: (0, nf + j)),
                  pl.BlockSpec((FFN_CONV, tn), lambda i, j: (0, j)),
                  pl.BlockSpec((1, tn), lambda i, j: (0, j))],
        out_specs=pl.BlockSpec((tm, tn), lambda i, j: (i, j)),
        out_shape=jax.ShapeDtypeStruct((n, f), BF16),
        scratch_shapes=[pltpu.VMEM((tm, d), BF16),
                        pltpu.VMEM((BF16_ROWS, d), BF16),
                        pltpu.VMEM((tm + BF16_ROWS, tn), F32)],
        compiler_params=_cparams("parallel", "arbitrary"),
        name="ffn_up_act",
    )(x2, x2, g.reshape(1, d), shift, scale, w_up, w_up, conv_w, conv_b.reshape(1, f))


def _linear_res_kernel(a_ref, w_ref, x_ref, gate_ref, o_ref):
    acc = jnp.dot(a_ref[...], w_ref[...], preferred_element_type=F32)
    o_ref[...] = x_ref[...] + gate_ref[0] * acc


def _linear_res(a, w, x2, gate, t_len, tm=1024, tn=512):
    n, k = a.shape
    m = w.shape[1]
    tm = min(tm, t_len)
    tpb = t_len // tm
    return pl.pallas_call(
        _linear_res_kernel,
        grid=(n // tm, m // tn),
        in_specs=[pl.BlockSpec((tm, k), lambda i, j: (i, 0)),
                  pl.BlockSpec((k, tn), lambda i, j: (0, j)),
                  pl.BlockSpec((tm, tn), lambda i, j: (i, j)),
                  pl.BlockSpec((1, 1, tn), lambda i, j: (i // tpb, 0, j))],
        out_specs=pl.BlockSpec((tm, tn), lambda i, j: (i, j)),
        out_shape=jax.ShapeDtypeStruct((n, m), F32),
        compiler_params=_cparams("parallel", "arbitrary"),
        name="linear_residual",
    )(a, w, x2, gate)


def _retention_kernel(q_ref, k_ref, v_ref, g_ref, cos_ref, sin_ref, dint_ref, dq_ref, dk_ref,
                      dc_ref, o_ref, s_ref, *, heads, dk, dv):
    chunk = pl.program_id(1)

    @pl.when(chunk == 0)
    def _():
        s_ref[...] = jnp.zeros_like(s_ref)

    cos = cos_ref[...]
    sin = sin_ref[...]
    half = dk // 2

    def rot(ref, h):
        x1 = ref[:, h * dk:h * dk + half].astype(F32)
        x2 = ref[:, h * dk + half:(h + 1) * dk].astype(F32)
        return jnp.concatenate([x1 * cos - x2 * sin, x1 * sin + x2 * cos], axis=-1)

    hs = range(heads)
    q = [rot(q_ref, h) for h in hs]
    k = [rot(k_ref, h) * (dk ** -0.5) for h in hs]
    v = [v_ref[:, h * dv:(h + 1) * dv] for h in hs]
    scores = [_mm_nt(q[h], k[h]) * dint_ref[h] for h in hs]
    state = [s_ref[h] for h in hs]
    out = [_mm(scores[h], v[h]) + _mm(q[h] * dq_ref[h], state[h]) for h in hs]
    for h in hs:
        s_ref[h] = dc_ref[h] * state[h] + _mm((k[h] * dk_ref[h]).T, v[h])
    for h in hs:
        oc = out[h] - jnp.mean(out[h], axis=-1, keepdims=True)
        o = oc * lax.rsqrt(jnp.mean(oc * oc, axis=-1, keepdims=True) + EPS)
        gate = g_ref[:, h * dv:(h + 1) * dv].astype(F32)
        o_ref[:, h * dv:(h + 1) * dv] = (o * _silu(gate)).astype(o_ref.dtype)


def _retention_tables(t_len, dk):
    half = dk // 2
    inv = ROPE_BASE ** (-np.arange(half, dtype=np.float64) / half)
    ang = np.arange(t_len, dtype=np.float64)[:, None] * inv[None, :]
    lg = np.log1p(-np.power(2.0, -5.0 - np.arange(RET_HEADS, dtype=np.float64)))
    idx = np.arange(RET_CHUNK, dtype=np.float64)
    diff = idx[:, None] - idx[None, :]
    dint = np.where(diff >= 0, np.exp(lg[:, None, None] * np.maximum(diff, 0.0)), 0.0)
    dq = np.exp(lg[:, None] * (idx + 1.0))[..., None]
    dkk = np.exp(lg[:, None] * (RET_CHUNK - 1.0 - idx))[..., None]
    dc = np.exp(lg * RET_CHUNK)[:, None, None]
    f = lambda a: jnp.asarray(a, F32)
    return f(np.cos(ang)), f(np.sin(ang)), f(dint), f(dq), f(dkk), f(dc)


def _retention(qkvg, n_b, t_len, d):
    n = qkvg.shape[0]
    heads = RET_HEADS
    dk = d // heads
    dv = 2 * dk
    lc = RET_CHUNK
    nc = t_len // lc
    cos, sin, dint, dq, dkk, dc = _retention_tables(t_len, dk)
    row = lambda b, c: b * nc + c
    full = lambda a: pl.BlockSpec(a.shape, lambda b, c: (0,) * a.ndim)
    kern = functools.partial(_retention_kernel, heads=heads, dk=dk, dv=dv)
    return pl.pallas_call(
        kern,
        grid=(n_b, nc),
        in_specs=[pl.BlockSpec((lc, d), lambda b, c: (row(b, c), 0)),
                  pl.BlockSpec((lc, d), lambda b, c: (row(b, c), 1)),
                  pl.BlockSpec((lc, 2 * d), lambda b, c: (row(b, c), 1)),
                  pl.BlockSpec((lc, 2 * d), lambda b, c: (row(b, c), 2)),
                  pl.BlockSpec((lc, dk // 2), lambda b, c: (c, 0)),
                  pl.BlockSpec((lc, dk // 2), lambda b, c: (c, 0)),
                  full(dint), full(dq), full(dkk), full(dc)],
        out_specs=pl.BlockSpec((lc, 2 * d), lambda b, c: (row(b, c), 0)),
        out_shape=jax.ShapeDtypeStruct((n, 2 * d), BF16),
        scratch_shapes=[pltpu.VMEM((heads, dk, dv), F32)],
        compiler_params=_cparams("parallel", "arbitrary"),
        name="retention_chunk",
    )(qkvg, qkvg, qkvg, qkvg, cos, sin, dint, dq, dkk, dc)


def _mlstm_kernel(q_ref, k_ref, v_ref, og_ref, gt_ref, cw_ref, cb_ref, gb_ref, ng_ref, tril_ref,
                  o_ref, qext, kext, c_ref, n_ref, m_ref, *, lc, dqk, dv, heads):
    chunk = pl.program_id(1)
    qw = heads * dqk

    @pl.when(chunk == 0)
    def _():
        qext[0:SUBLANES, :] = jnp.zeros((SUBLANES, qw), F32)
        kext[0:SUBLANES, :] = jnp.zeros((SUBLANES, qw), F32)
        c_ref[...] = jnp.zeros_like(c_ref)
        n_ref[...] = jnp.zeros_like(n_ref)
        m_ref[...] = jnp.zeros_like(m_ref)

    def conv_silu(raw_ref, ext, c0):
        ext[SUBLANES:SUBLANES + lc, :] = raw_ref[...]
        y = cb_ref[:, c0:c0 + qw]
        for j in range(MLSTM_CONV):
            off = SUBLANES - (MLSTM_CONV - 1) + j
            y = y + cw_ref[j:j + 1, c0:c0 + qw] * ext[off:off + lc, :]
        ext[0:SUBLANES, :] = ext[lc:lc + SUBLANES, :]
        return _silu(y)

    q_all = conv_silu(q_ref, qext, 0)
    k_all = conv_silu(k_ref, kext, qw) * (dqk ** -0.5)

    gt = gt_ref[...] + gb_ref[...]
    lane = lax.broadcasted_iota(jnp.int32, gt.shape, 1)
    tril = tril_ref[...]
    causal = (lax.broadcasted_iota(jnp.int32, (lc, lc), 0)
              >= lax.broadcasted_iota(jnp.int32, (lc, lc), 1))

    hs = range(heads)
    q = [q_all[:, h * dqk:(h + 1) * dqk] for h in hs]
    k = [k_all[:, h * dqk:(h + 1) * dqk] for h in hs]
    v = [v_ref[:, h * dv:(h + 1) * dv] for h in hs]
    ipre = [jnp.sum(jnp.where(lane == h, gt, 0.0), axis=-1, keepdims=True) for h in hs]
    fpre = [jnp.sum(jnp.where(lane == h + heads, gt, 0.0), axis=-1, keepdims=True) for h in hs]
    li = [GATE_CAP * jnp.tanh(x / GATE_CAP) for x in ipre]
    lf = [-_softplus(-(GATE_CAP * jnp.tanh(x / GATE_CAP))) for x in fpre]
    b_full = [_mm_split_rhs(tril, jnp.broadcast_to(x, (lc, lc))) for x in lf]
    b_col = [x[:, 0:1] for x in b_full]
    dmat = [jnp.where(causal, b_full[h] - b_full[h].T + jnp.broadcast_to(li[h], (lc, lc)).T,
                      NEG_BIG) for h in hs]
    m_st = [m_ref[h] for h in hs]
    m_inter = [b_col[h] + m_st[h] for h in hs]
    m_t = [jnp.maximum(m_inter[h], jnp.max(dmat[h], axis=-1, keepdims=True)) for h in hs]
    scores = [_mm_nt(q[h], k[h]) * jnp.exp(dmat[h] - m_t[h]) for h in hs]
    w_inter = [jnp.exp(m_inter[h] - m_t[h]) for h in hs]
    c_st = [c_ref[h] for h in hs]
    n_st = [n_ref[h] for h in hs]
    num = [_mm(scores[h], v[h]) + w_inter[h] * _mm(q[h], c_st[h]) for h in hs]
    den = [jnp.sum(scores[h], axis=-1, keepdims=True)
           + w_inter[h] * jnp.sum(q[h] * n_st[h], axis=-1, keepdims=True) for h in hs]
    for h in hs:
        b_last = b_col[h][lc - 1:lc, :]
        gdec = b_last - b_col[h] + li[h]
        m_new = jnp.maximum(b_last + m_st[h], jnp.max(gdec, axis=0, keepdims=True))
        wk = jnp.exp(gdec - m_new)
        carry = jnp.exp(b_last + m_st[h] - m_new)
        kw = k[h] * wk
        c_ref[h] = carry * c_st[h] + _mm(kw.T, v[h])
        n_ref[h] = carry * n_st[h] + jnp.sum(kw, axis=0, keepdims=True)
        m_ref[h] = m_new
    for h in hs:
        hh = num[h] / jnp.maximum(jnp.abs(den[h]), jnp.exp(-m_t[h]))
        cs = slice(h * dv, (h + 1) * dv)
        hn = hh * lax.rsqrt(jnp.mean(hh * hh, axis=-1, keepdims=True) + EPS) * ng_ref[:, cs]
        o_ref[:, cs] = (hn * jax.nn.sigmoid(og_ref[:, cs])).astype(o_ref.dtype)


def _mlstm(proj, conv_w, conv_b, gate_b, norm_g, n_b, t_len, d):
    n = proj.shape[0]
    heads = MLSTM_HEADS
    dqk = d // (2 * heads)
    dv = d // heads
    lc = MLSTM_CHUNK
    nc = t_len // lc
    qw = heads * dqk
    gtb = (2 * qw + 2 * d) // LANES
    gb = jnp.zeros((1, LANES), F32).at[0, :2 * heads].set(gate_b.astype(F32))
    tril = jnp.asarray(np.tril(np.ones((lc, lc), np.float32)), BF16)
    kern = functools.partial(_mlstm_kernel, lc=lc, dqk=dqk, dv=dv, heads=heads)
    row = lambda b, c: b * nc + c
    full = lambda a: pl.BlockSpec(a.shape, lambda b, c: (0,) * a.ndim)
    cb = conv_b.reshape(1, 2 * qw)
    ng = norm_g.reshape(1, d)
    return pl.pallas_call(
        kern,
        grid=(n_b, nc),
        in_specs=[pl.BlockSpec((lc, qw), lambda b, c: (row(b, c), 0)),
                  pl.BlockSpec((lc, qw), lambda b, c: (row(b, c), 1)),
                  pl.BlockSpec((lc, d), lambda b, c: (row(b, c), (2 * qw) // d)),
                  pl.BlockSpec((lc, d), lambda b, c: (row(b, c), (2 * qw) // d + 1)),
                  pl.BlockSpec((lc, LANES), lambda b, c: (row(b, c), gtb)),
                  full(conv_w), full(cb), full(gb), full(ng), full(tril)],
        out_specs=pl.BlockSpec((lc, d), lambda b, c: (row(b, c), 0)),
        out_shape=jax.ShapeDtypeStruct((n, d), BF16),
        scratch_shapes=[pltpu.VMEM((lc + SUBLANES, qw), F32),
                        pltpu.VMEM((lc + SUBLANES, qw), F32),
                        pltpu.VMEM((heads, dqk, dv), F32),
                        pltpu.VMEM((heads, 1, dqk), F32),
                        pltpu.VMEM((heads, 1, 1), F32)],
        compiler_params=_cparams("parallel", "arbitrary"),
        name="mlstm_chunk",
    )(proj, proj, proj, proj, proj, conv_w, cb, gb, ng, tril)


def _rwkv_prep_kernel(p_ref, prev_ref, mu_ref, w0_ref, a0_ref, kk_ref, ka_ref, w2_ref, a2_ref,
                      g2_ref, ones_ref, r_out, wl_out, k_out, v_out, kn_out, ag_out, g_out,
                      ext, *, tm, d, tpb):
    first = (pl.program_id(0) % tpb) == 0

    def shifted(c0, width):
        outs = []
        for s0 in range(c0, c0 + width, LANES):
            cs = slice(s0, s0 + LANES)
            ext[0:SUBLANES, :] = jnp.where(first, 0.0, prev_ref[:, cs])
            ext[SUBLANES:SUBLANES + tm, :] = p_ref[:, cs]
            cur = p_ref[:, cs]
            outs.append(cur + mu_ref[:, cs] * (ext[SUBLANES - 1:SUBLANES - 1 + tm, :] - cur))
        return outs[0] if len(outs) == 1 else jnp.concatenate(outs, axis=-1)

    wlo = jnp.tanh(shifted(3 * d, LANES)).astype(BF16)
    alo = shifted(3 * d + LANES, LANES).astype(BF16)
    glo = jax.nn.sigmoid(shifted(3 * d + 2 * LANES, RWKV_GATE_LORA)).astype(BF16)
    ones = ones_ref[...]

    for j in range(d // LANES):
        sl = slice(j * LANES, (j + 1) * LANES)
        r = shifted(j * LANES, LANES)
        k = shifted(d + j * LANES, LANES)
        v = shifted(2 * d + j * LANES, LANES)
        wraw = w0_ref[:, sl] + jnp.dot(wlo, w2_ref[:, sl], preferred_element_type=F32)
        wfin = -_softplus(-wraw) - 0.5
        ag = jax.nn.sigmoid(a0_ref[:, sl] + jnp.dot(alo, a2_ref[:, sl],
                                                    preferred_element_type=F32))
        kn = k * kk_ref[:, sl]
        ss = _mm_split_lhs(kn * kn, ones)
        kn = kn / jnp.maximum(jnp.sqrt(ss), 1e-12)
        r_out[0, j] = r.astype(r_out.dtype)
        wl_out[0, j] = -jnp.exp(wfin)
        k_out[0, j] = (k * (1.0 + (ag - 1.0) * ka_ref[:, sl])).astype(k_out.dtype)
        v_out[0, j] = v.astype(v_out.dtype)
        kn_out[0, j] = kn.astype(kn_out.dtype)
        ag_out[0, j] = ag.astype(ag_out.dtype)
        g_out[:, sl] = jnp.dot(glo, g2_ref[:, sl],
                               preferred_element_type=F32).astype(g_out.dtype)


def _pair_ones(scale=1.0):
    blk = np.kron(np.eye(2, dtype=np.float32), np.ones((RWKV_HEAD, RWKV_HEAD), np.float32))
    return jnp.asarray(blk * scale, BF16)


def _rwkv_prep(p, mu, w0, a0, k_k, k_a, w2, a2, g2, n_b, t_len, d, tm=256):
    n, pw = p.shape
    tm = min(tm, t_len)
    tpb = t_len // tm
    rb = tm // SUBLANES
    npair = d // LANES
    pair_shape = lambda dt: jax.ShapeDtypeStruct((n_b, npair, t_len, LANES), dt)
    pair_spec = pl.BlockSpec((1, npair, tm, LANES), lambda i: (i // tpb, 0, i % tpb, 0))
    vec = lambda a: a.reshape(1, -1)
    cvec = lambda w: pl.BlockSpec((1, w), lambda i: (0, 0))
    kern = functools.partial(_rwkv_prep_kernel, tm=tm, d=d, tpb=tpb)
    pair_dtypes = [BF16, F32, BF16, BF16, BF16, BF16]
    return pl.pallas_call(
        kern,
        grid=(n // tm,),
        in_specs=[pl.BlockSpec((tm, pw), lambda i: (i, 0)),
                  pl.BlockSpec((SUBLANES, pw), lambda i: (jnp.maximum(i * rb - 1, 0), 0)),
                  cvec(pw), cvec(d), cvec(d), cvec(d), cvec(d),
                  pl.BlockSpec((LANES, d), lambda i: (0, 0)),
                  pl.BlockSpec((LANES, d), lambda i: (0, 0)),
                  pl.BlockSpec((RWKV_GATE_LORA, d), lambda i: (0, 0)),
                  pl.BlockSpec((LANES, LANES), lambda i: (0, 0))],
        out_specs=[pair_spec] * 6 + [pl.BlockSpec((tm, d), lambda i: (i, 0))],
        out_shape=[pair_shape(dt) for dt in pair_dtypes] + [jax.ShapeDtypeStruct((n, d), BF16)],
        scratch_shapes=[pltpu.VMEM((tm + SUBLANES, LANES), F32)],
        compiler_params=_cparams("parallel"),
        name="rwkv_prep",
    )(p, p, vec(mu), vec(w0), vec(a0), vec(k_k), vec(k_a), w2, a2, g2, _pair_ones())


def _rwkv_scan_kernel(r_ref, wl_ref, k_ref, v_ref, kn_ref, ag_ref, g_ref, lnw_ref, lnb_ref,
                      rk_ref, tril_ref, ms_ref, mi_ref, eye_ref, avg_ref, ones_ref,
                      o_ref, h_ref, y_ref, *, lc, nchunk, npp):
    block = pl.program_id(2)

    @pl.when(block == 0)
    def _():
        h_ref[...] = jnp.zeros_like(h_ref)

    lane = lax.broadcasted_iota(jnp.int32, (lc, LANES), 1)
    head0 = lane < RWKV_HEAD

    def stack(x):
        return jnp.concatenate([jnp.where(head0, x, 0.0), jnp.where(head0, 0.0, x)], axis=0)

    tril = tril_ref[...]
    ms = ms_ref[...]
    mi = mi_ref[...]
    eye = eye_ref[...]
    two = 2 * lc
    probs = [(pi, ci) for ci in range(nchunk) for pi in range(npp)]
    ps = range(len(probs))

    def ld(ref):
        return [ref[0, pi, ci * lc:(ci + 1) * lc, :].astype(F32) for pi, ci in probs]

    wl = ld(wl_ref)
    cum = [_mm_split_rhs(tril, x) for x in wl]
    cl = [x[lc - 1:lc, :] for x in cum]
    r, k, v, kn, ag = ld(r_ref), ld(k_ref), ld(v_ref), ld(kn_ref), ld(ag_ref)
    bb = [kn[i] * ag[i] for i in ps]
    e_n = [jnp.exp(-cum[i]) for i in ps]
    e_l = [jnp.exp(cl[i] - cum[i]) for i in ps]
    a_s = [stack(-kn[i] * jnp.exp(cum[i] - wl[i])) for i in ps]
    r_s = [stack(r[i] * jnp.exp(cum[i])) for i in ps]
    b_s = [stack(bb[i] * e_n[i]) for i in ps]
    k_s = [stack(k[i] * e_n[i]) for i in ps]
    v_s = [stack(v[i]) for i in ps]
    bp_t = [stack(bb[i] * e_l[i]).T for i in ps]
    kp_t = [stack(k[i] * e_l[i]).T for i in ps]

    gm = [_mm_nt(jnp.concatenate([a_s[i], r_s[i]], axis=0),
                 jnp.concatenate([b_s[i], k_s[i]], axis=0)) for i in ps]
    npow = [gm[i][:two, :two] * ms for i in ps]
    aak_v = [_mm(gm[i][:two, two:] * ms, v_s[i]) for i in ps]
    arb = [gm[i][two:, :two] * mi for i in ps]
    ark_v = [_mm(gm[i][two:, two:] * mi, v_s[i]) for i in ps]
    kp_v = [_mm(kp_t[i], v_s[i]) for i in ps]

    tinv = [eye + npow[i] for i in ps]
    for _ in range(int(np.log2(lc)) - 1):
        npow = [_mm(npow[i], npow[i]) for i in ps]
        tinv = [_mm(tinv[i], eye + npow[i]) for i in ps]

    taw = [_mm(tinv[i], jnp.concatenate([a_s[i], aak_v[i]], axis=1)) for i in ps]
    arb_taw = [_mm(arb[i], taw[i]) for i in ps]
    bp_taw = [_mm(bp_t[i], taw[i]) for i in ps]
    qh = [r_s[i] + arb_taw[i][:, :LANES] for i in ps]
    yin = [arb_taw[i][:, LANES:] + ark_v[i] for i in ps]
    phi = [eye * jnp.exp(cl[i]) + bp_taw[i][:, :LANES] for i in ps]
    psi = [bp_taw[i][:, LANES:] + kp_v[i] for i in ps]

    hst = [h_ref[pi] for pi in range(npp)]
    for i, (pi, ci) in enumerate(probs):
        ys = _mm(qh[i], hst[pi]) + yin[i]
        hst[pi] = _mm(phi[i], hst[pi]) + psi[i]
        y_ref[pi, ci * lc:(ci + 1) * lc, :] = ys[:lc] + ys[lc:]
    for pi in range(npp):
        h_ref[pi] = hst[pi]

    avg = avg_ref[...]
    ones = ones_ref[...]
    for pi in range(npp):
        cs = slice(pi * LANES, (pi + 1) * LANES)
        y = y_ref[pi]
        yc = y - _mm_split_lhs(y, avg)
        var = _mm_split_lhs(yc * yc, avg)
        yn = yc * lax.rsqrt(var + RWKV_LN_EPS) * lnw_ref[:, cs] + lnb_ref[:, cs]
        rkk = r_ref[0, pi].astype(F32) * k_ref[0, pi].astype(F32) * rk_ref[:, cs]
        rk = _mm_split_lhs(rkk, ones)
        out = (yn + rk * v_ref[0, pi].astype(F32)) * g_ref[:, cs].astype(F32)
        o_ref[:, cs] = out.astype(o_ref.dtype)


def _rwkv_scan(r, wl, k, v, kn, ag, g, ln_w, ln_b, r_k, n_b, t_len, d, nchunk=4, npp=2):
    lc = RWKV_CHUNK
    lb = lc * nchunk
    nb = t_len // lb
    npair = d // LANES
    two = 2 * lc
    pw = npp * LANES
    idx = np.arange(two)
    same = (idx[:, None] // lc) == (idx[None, :] // lc)
    ms = jnp.asarray((same & (idx[:, None] > idx[None, :])).astype(np.float32))
    mi = jnp.asarray((same & (idx[:, None] >= idx[None, :])).astype(np.float32))
    eye = jnp.asarray(np.eye(two, dtype=np.float32))
    tril = jnp.asarray(np.tril(np.ones((lc, lc), np.float32)), BF16)
    pair_spec = pl.BlockSpec((1, npp, lb, LANES), lambda b, p, t: (b, p, t, 0))
    vec_spec = pl.BlockSpec((1, pw), lambda b, p, t: (0, p))
    sq = lambda s: pl.BlockSpec((s, s), lambda b, p, t: (0, 0))
    kern = functools.partial(_rwkv_scan_kernel, lc=lc, nchunk=nchunk, npp=npp)
    return pl.pallas_call(
        kern,
        grid=(n_b, npair // npp, nb),
        in_specs=[pair_spec] * 6
        + [pl.BlockSpec((lb, pw), lambda b, p, t: (b * nb + t, p)),
           vec_spec, vec_spec, vec_spec, sq(lc), sq(two), sq(two), sq(two), sq(LANES), sq(LANES)],
        out_specs=pl.BlockSpec((lb, pw), lambda b, p, t: (b * nb + t, p)),
        out_shape=jax.ShapeDtypeStruct((n_b * t_len, d), BF16),
        scratch_shapes=[pltpu.VMEM((npp, LANES, LANES), F32),
                        pltpu.VMEM((npp, lb, LANES), F32)],
        compiler_params=_cparams("parallel", "parallel", "arbitrary"),
        name="rwkv_scan",
    )(r, wl, k, v, kn, ag, g, ln_w.reshape(1, d), ln_b.reshape(1, d), r_k.reshape(1, d),
      tril, ms, mi, eye, _pair_ones(1.0 / RWKV_HEAD), _pair_ones())


def _final_kernel(x_ref, g_ref, sh_ref, sc_ref, o_ref):
    o_ref[...] = _rms_mod(x_ref[...], g_ref[...], sh_ref[0], sc_ref[0])


def _final_norm(x2, g, shift, scale, t_len, tm=256):
    n, d = x2.shape
    tm = min(tm, t_len)
    tpb = t_len // tm
    return pl.pallas_call(
        _final_kernel,
        grid=(n // tm,),
        in_specs=[pl.BlockSpec((tm, d), lambda i: (i, 0)),
                  pl.BlockSpec((1, d), lambda i: (0, 0)),
                  pl.BlockSpec((1, 1, d), lambda i: (i // tpb, 0, 0)),
                  pl.BlockSpec((1, 1, d), lambda i: (i // tpb, 0, 0))],
        out_specs=pl.BlockSpec((tm, d), lambda i: (i, 0)),
        out_shape=jax.ShapeDtypeStruct((n, d), F32),
        compiler_params=_cparams("parallel"),
        name="final_norm",
    )(x2, g.reshape(1, d), shift, scale)


def _pad_cols(w, width):
    return jnp.pad(w, ((0, 0), (0, width - w.shape[1])))


def _rwkv_pack_in(w_in, mu, d):
    c0 = 3 * d
    c1 = c0 + RWKV_DECAY_LORA
    c2 = c1 + RWKV_A_LORA
    def pack(a):
        return jnp.concatenate([a[..., :c0], _pad_cols(a[..., c0:c1], LANES),
                                _pad_cols(a[..., c1:c2], LANES), a[..., c2:]], axis=-1)
    return pack(w_in), pack(mu.reshape(1, -1))[0]


def _pad_rows(w, rows):
    return jnp.pad(w, ((0, rows - w.shape[0]), (0, 0)))


def kernel(x, c, mod_w, mod_b, norm_mix_g, norm_ffn_g, ret_w_in, ret_w_out, rwkv_w_in, rwkv_mu, rwkv_w0, rwkv_w2, rwkv_a0, rwkv_a2, rwkv_g2, rwkv_k_k, rwkv_k_a, rwkv_r_k, rwkv_ln_w, rwkv_ln_b, rwkv_w_out, mlstm_w_in, mlstm_conv_w, mlstm_conv_b, mlstm_gate_b, mlstm_norm_g, mlstm_w_out, ffn_w_up, ffn_conv_w, ffn_conv_b, ffn_w_down, final_g, final_mod_w, final_mod_b):
    n_b, t_len, d = x.shape
    depth = mod_w.shape[0]
    x2 = x.reshape(n_b * t_len, d)

    c_pad = jnp.pad(c, ((0, SUBLANES - n_b), (0, 0)))
    mod = _mod_vectors(c_pad, mod_w, mod_b)[:, :n_b]
    fmod = _mod_vectors(c_pad, final_mod_w[None], final_mod_b[None])[0, :n_b]

    def mvec(i, j):
        return mod[i, :, j * d:(j + 1) * d][:, None, :]

    for i in range(depth):
        sh_m, sc_m, g_m, sh_f, sc_f, g_f = (mvec(i, j) for j in range(6))
        kind, j = i % N_MIXERS, i // N_MIXERS
        if kind == 0:
            qkvg = _nm_linear(x2, norm_mix_g[i], sh_m, sc_m, ret_w_in[j].astype(BF16), t_len, BF16)
            o = _retention(qkvg, n_b, t_len, d)
            x2 = _linear_res(o, ret_w_out[j].astype(BF16), x2, g_m, t_len)
        elif kind == 1:
            w_in, mu = _rwkv_pack_in(rwkv_w_in[j], rwkv_mu[j], d)
            p = _nm_linear(x2, norm_mix_g[i], sh_m, sc_m, w_in.astype(BF16), t_len, F32)
            r, wl, k, v, kn, ag, g = _rwkv_prep(
                p, mu, rwkv_w0[j], rwkv_a0[j], rwkv_k_k[j], rwkv_k_a[j],
                _pad_rows(rwkv_w2[j], LANES).astype(BF16), _pad_rows(rwkv_a2[j], LANES).astype(BF16),
                rwkv_g2[j].astype(BF16), n_b, t_len, d)
            o = _rwkv_scan(r, wl, k, v, kn, ag, g, rwkv_ln_w[j], rwkv_ln_b[j], rwkv_r_k[j],
                           n_b, t_len, d)
            x2 = _linear_res(o, rwkv_w_out[j].astype(BF16), x2, g_m, t_len)
        else:
            w_in = mlstm_w_in[j]
            w_in = _pad_cols(w_in, -(-w_in.shape[1] // 512) * 512)
            proj = _nm_linear(x2, norm_mix_g[i], sh_m, sc_m, w_in.astype(BF16), t_len, F32)
            o = _mlstm(proj, mlstm_conv_w[j], mlstm_conv_b[j], mlstm_gate_b[j], mlstm_norm_g[j],
                       n_b, t_len, d)
            x2 = _linear_res(o, mlstm_w_out[j].astype(BF16), x2, g_m, t_len)
        act = _ffn_up(x2, norm_ffn_g[i], sh_f, sc_f, ffn_w_up[i].astype(BF16), ffn_conv_w[i],
                      ffn_conv_b[i], t_len)
        x2 = _linear_res(act, ffn_w_down[i].astype(BF16), x2, g_f, t_len)

    out = _final_norm(x2, final_g, fmod[:, :d][:, None, :], fmod[:, d:][:, None, :], t_len)
    return out.reshape(n_b, t_len, d)
```

```python
import functools

import numpy as np
import jax
import jax.numpy as jnp
from jax import lax
from jax.experimental import pallas as pl
from jax.experimental.pallas import tpu as pltpu

F32 = jnp.float32
BF16 = jnp.bfloat16

V7X_VMEM_BYTES = 64 * 1024 * 1024
VMEM_LIMIT = (V7X_VMEM_BYTES * 7) // 8
LANES = 128
SUBLANES = 8
BF16_ROWS = 16
NORM_ROWS = 16

EPS = 1e-6
N_MIXERS = 3

RET_HEADS = 8
ROPE_BASE = 10000.0
RET_CHUNK = 128

RWKV_HEAD = 64
RWKV_DECAY_LORA = 96
RWKV_A_LORA = 96
RWKV_GATE_LORA = 256
RWKV_LORA_W = 2 * LANES + RWKV_GATE_LORA
RWKV_LN_EPS = 64e-5
RWKV_CHUNK = 64

MLSTM_HEADS = 4
MLSTM_CONV = 4
MLSTM_CHUNK = 128
GATE_CAP = 15.0

FFN_CONV = 3
NEG_BIG = -1e30


def _cparams(*sem):
    return pltpu.CompilerParams(dimension_semantics=sem, vmem_limit_bytes=VMEM_LIMIT)


def _mm(a, b):
    return jnp.dot(a.astype(BF16), b.astype(BF16), preferred_element_type=F32)


def _mm_nt(a, b):
    return lax.dot_general(a.astype(BF16), b.astype(BF16), (((1,), (1,)), ((), ())),
                           preferred_element_type=F32)


def _split(x):
    hi = x.astype(BF16)
    lo = (x - hi.astype(F32)).astype(BF16)
    return hi, lo


def _mm_split_rhs(m, x):
    hi, lo = _split(x)
    return (jnp.dot(m, hi, preferred_element_type=F32)
            + jnp.dot(m, lo, preferred_element_type=F32))


def _silu(x):
    return x * jax.nn.sigmoid(x)


def _softplus(z):
    return jnp.maximum(z, 0.0) + jnp.log(1.0 + jnp.exp(-jnp.abs(z)))


def _rms_mod(x, g, shift, scale):
    y = x * lax.rsqrt(jnp.mean(x * x, axis=-1, keepdims=True) + EPS)
    return (y * g) * (1.0 + scale) + shift


def _mod_kernel(c_ref, w_ref, b_ref, o_ref):
    ca = _silu(c_ref[...])
    o_ref[0] = jnp.dot(ca, w_ref[0], precision=lax.Precision.HIGHEST,
                       preferred_element_type=F32) + b_ref[0]


def _mod_vectors(c_pad, w, b, tn=1024):
    depth, d, n = w.shape
    rows = c_pad.shape[0]
    return pl.pallas_call(
        _mod_kernel,
        grid=(depth, n // tn),
        in_specs=[pl.BlockSpec((rows, d), lambda i, j: (0, 0)),
                  pl.BlockSpec((1, d, tn), lambda i, j: (i, 0, j)),
                  pl.BlockSpec((1, 1, tn), lambda i, j: (i, 0, j))],
        out_specs=pl.BlockSpec((1, rows, tn), lambda i, j: (i, 0, j)),
        out_shape=jax.ShapeDtypeStruct((depth, rows, n), F32),
        compiler_params=_cparams("parallel", "parallel"),
        name="mod_vectors",
    )(c_pad, w, b.reshape(depth, 1, n))


def _nm_linear_kernel(x_ref, g_ref, sh_ref, sc_ref, w_ref, o_ref, h_ref, *, tm, rchunk):
    col_tile = pl.program_id(1)

    @pl.when(col_tile == 0)
    def _():
        for r0 in range(0, tm, rchunk):
            h = _rms_mod(x_ref[r0:r0 + rchunk, :], g_ref[...], sh_ref[0], sc_ref[0])
            h_ref[r0:r0 + rchunk, :] = h.astype(BF16)

    o_ref[...] = jnp.dot(h_ref[...], w_ref[...].astype(BF16),
                         preferred_element_type=F32).astype(o_ref.dtype)


def _nm_linear(x2, g, shift, scale, w, layer, t_len, out_dtype, tm=1024):
    n, d = x2.shape
    m = w.shape[2]
    tn = 1024 if m % 1024 == 0 else 512
    tm = min(tm, t_len)
    tpb = t_len // tm
    kern = functools.partial(_nm_linear_kernel, tm=tm, rchunk=min(NORM_ROWS, tm))
    return pl.pallas_call(
        kern,
        grid=(n // tm, m // tn),
        in_specs=[pl.BlockSpec((tm, d), lambda i, j: (i, 0)),
                  pl.BlockSpec((1, d), lambda i, j: (0, 0)),
                  pl.BlockSpec((1, 1, d), lambda i, j: (i // tpb, 0, 0)),
                  pl.BlockSpec((1, 1, d), lambda i, j: (i // tpb, 0, 0)),
                  pl.BlockSpec((None, d, tn), lambda i, j: (layer, 0, j))],
        out_specs=pl.BlockSpec((tm, tn), lambda i, j: (i, j)),
        out_shape=jax.ShapeDtypeStruct((n, m), out_dtype),
        scratch_shapes=[pltpu.VMEM((tm, d), BF16)],
        compiler_params=_cparams("parallel", "arbitrary"),
        name="norm_mod_linear",
    )(x2, g.reshape(1, d), shift, scale, w)


def _ffn_up_kernel(x_ref, xp_ref, g_ref, sh_ref, sc_ref, wg_ref, wv_ref, cw_ref, cb_ref,
                   o_ref, h_ref, ext_ref, *, tm, rchunk, tpb):
    first = (pl.program_id(0) % tpb) == 0
    col_tile = pl.program_id(1)
    halo = BF16_ROWS

    @pl.when(col_tile == 0)
    def _():
        hp = _rms_mod(xp_ref[...], g_ref[...], sh_ref[0], sc_ref[0])
        h_ref[0:halo, :] = hp.astype(BF16)
        for r0 in range(0, tm, NORM_ROWS):
            h = _rms_mod(x_ref[r0:r0 + NORM_ROWS, :], g_ref[...], sh_ref[0], sc_ref[0])
            h_ref[halo + r0:halo + r0 + NORM_ROWS, :] = h.astype(BF16)

    ext_ref[...] = jnp.dot(h_ref[...], wg_ref[...].astype(BF16), preferred_element_type=F32)
    ext_ref[0:halo, :] = jnp.where(first, 0.0, ext_ref[0:halo, :])
    val = jnp.dot(h_ref[halo:halo + tm, :], wv_ref[...].astype(BF16),
                  preferred_element_type=F32)
    for r0 in range(0, tm, rchunk):
        base = halo + r0
        conv = (cw_ref[0:1, :] * ext_ref[base - 2:base - 2 + rchunk, :]
                + cw_ref[1:2, :] * ext_ref[base - 1:base - 1 + rchunk, :]
                + cw_ref[2:3, :] * ext_ref[base:base + rchunk, :]
                + cb_ref[...])
        o_ref[r0:r0 + rchunk, :] = (_silu(conv) * val[r0:r0 + rchunk, :]).astype(o_ref.dtype)


def _ffn_up(x2, g, shift, scale, w_up, layer, conv_w, conv_b, t_len, tm=1024, tn=512):
    n, d = x2.shape
    f = w_up.shape[2] // 2
    tm = min(tm, t_len)
    tpb = t_len // tm
    rb = tm // BF16_ROWS
    nf = f // tn
    kern = functools.partial(_ffn_up_kernel, tm=tm, rchunk=min(256, tm), tpb=tpb)
    return pl.pallas_call(
        kern,
        grid=(n // tm, nf),
        in_specs=[pl.BlockSpec((tm, d), lambda i, j: (i, 0)),
                  pl.BlockSpec((BF16_ROWS, d), lambda i, j: (jnp.maximum(i * rb - 1, 0), 0)),
                  pl.BlockSpec((1, d), lambda i, j: (0, 0)),
                  pl.BlockSpec((1, 1, d), lambda i, j: (i // tpb, 0, 0)),
                  pl.BlockSpec((1, 1, d), lambda i, j: (i // tpb, 0, 0)),
                  pl.BlockSpec((None, d, tn), lambda i, j: (layer, 0, j)),
                  pl.BlockSpec((None, d, tn), lambda i, j: (layer, 0, nf + j)),
                  pl.BlockSpec((FFN_CONV, tn), lambda i, j: (0, j)),
                  pl.BlockSpec((1, tn), lambda i, j: (0, j))],
        out_specs=pl.BlockSpec((tm, tn), lambda i, j: (i, j)),
        out_shape=jax.ShapeDtypeStruct((n, f), BF16),
        scratch_shapes=[pltpu.VMEM((tm + BF16_ROWS, d), BF16),
                        pltpu.VMEM((tm + BF16_ROWS, tn), F32)],
        compiler_params=_cparams("parallel", "arbitrary"),
        name="ffn_up_act",
    )(x2, x2, g.reshape(1, d), shift, scale, w_up, w_up, conv_w, conv_b.reshape(1, f))


def _linear_res_kernel(a_ref, w_ref, x_ref, gate_ref, o_ref):
    acc = jnp.dot(a_ref[...], w_ref[...], preferred_element_type=F32)
    o_ref[...] = x_ref[...] + gate_ref[0] * acc


def _linear_res(a, w, x2, gate, t_len, tm=1024, tn=512):
    n, k = a.shape
    m = w.shape[1]
    tm = min(tm, t_len)
    tpb = t_len // tm
    return pl.pallas_call(
        _linear_res_kernel,
        grid=(n // tm, m // tn),
        in_specs=[pl.BlockSpec((tm, k), lambda i, j: (i, 0)),
                  pl.BlockSpec((k, tn), lambda i, j: (0, j)),
                  pl.BlockSpec((tm, tn), lambda i, j: (i, j)),
                  pl.BlockSpec((1, 1, tn), lambda i, j: (i // tpb, 0, j))],
        out_specs=pl.BlockSpec((tm, tn), lambda i, j: (i, j)),
        out_shape=jax.ShapeDtypeStruct((n, m), F32),
        compiler_params=_cparams("parallel", "arbitrary"),
        name="linear_residual",
    )(a, w, x2, gate)


def _retention_kernel(q_ref, k_ref, v_ref, g_ref, cos_ref, sin_ref, dint_ref, dq_ref, dk_ref,
                      dc_ref, o_ref, s_ref, *, heads, dk, dv):
    chunk = pl.program_id(1)

    @pl.when(chunk == 0)
    def _():
        s_ref[...] = jnp.zeros_like(s_ref)

    cos = cos_ref[...]
    sin = sin_ref[...]
    half = dk // 2

    def rot(ref, h):
        x1 = ref[:, h * dk:h * dk + half].astype(F32)
        x2 = ref[:, h * dk + half:(h + 1) * dk].astype(F32)
        return jnp.concatenate([x1 * cos - x2 * sin, x1 * sin + x2 * cos], axis=-1)

    hs = range(heads)
    q = [rot(q_ref, h) for h in hs]
    k = [rot(k_ref, h) * (dk ** -0.5) for h in hs]
    v = [v_ref[:, h * dv:(h + 1) * dv] for h in hs]
    scores = [_mm_nt(q[h], k[h]) * dint_ref[h] for h in hs]
    state = [s_ref[h] for h in hs]
    out = [_mm(scores[h], v[h]) + _mm(q[h] * dq_ref[h], state[h]) for h in hs]
    for h in hs:
        s_ref[h] = dc_ref[h] * state[h] + _mm((k[h] * dk_ref[h]).T, v[h])
    for h in hs:
        oc = out[h] - jnp.mean(out[h], axis=-1, keepdims=True)
        o = oc * lax.rsqrt(jnp.mean(oc * oc, axis=-1, keepdims=True) + EPS)
        gate = g_ref[:, h * dv:(h + 1) * dv].astype(F32)
        o_ref[:, h * dv:(h + 1) * dv] = (o * _silu(gate)).astype(o_ref.dtype)


def _retention_tables(t_len, dk):
    half = dk // 2
    inv = ROPE_BASE ** (-np.arange(half, dtype=np.float64) / half)
    ang = np.arange(t_len, dtype=np.float64)[:, None] * inv[None, :]
    lg = np.log1p(-np.power(2.0, -5.0 - np.arange(RET_HEADS, dtype=np.float64)))
    idx = np.arange(RET_CHUNK, dtype=np.float64)
    diff = idx[:, None] - idx[None, :]
    dint = np.where(diff >= 0, np.exp(lg[:, None, None] * np.maximum(diff, 0.0)), 0.0)
    dq = np.exp(lg[:, None] * (idx + 1.0))[..., None]
    dkk = np.exp(lg[:, None] * (RET_CHUNK - 1.0 - idx))[..., None]
    dc = np.exp(lg * RET_CHUNK)[:, None, None]
    f = lambda a: jnp.asarray(a, F32)
    return f(np.cos(ang)), f(np.sin(ang)), f(dint), f(dq), f(dkk), f(dc)


def _retention(qkvg, n_b, t_len, d):
    n = qkvg.shape[0]
    heads = RET_HEADS
    dk = d // heads
    dv = 2 * dk
    lc = RET_CHUNK
    nc = t_len // lc
    cos, sin, dint, dq, dkk, dc = _retention_tables(t_len, dk)
    row = lambda b, c: b * nc + c
    full = lambda a: pl.BlockSpec(a.shape, lambda b, c: (0,) * a.ndim)
    kern = functools.partial(_retention_kernel, heads=heads, dk=dk, dv=dv)
    return pl.pallas_call(
        kern,
        grid=(n_b, nc),
        in_specs=[pl.BlockSpec((lc, d), lambda b, c: (row(b, c), 0)),
                  pl.BlockSpec((lc, d), lambda b, c: (row(b, c), 1)),
                  pl.BlockSpec((lc, 2 * d), lambda b, c: (row(b, c), 1)),
                  pl.BlockSpec((lc, 2 * d), lambda b, c: (row(b, c), 2)),
                  pl.BlockSpec((lc, dk // 2), lambda b, c: (c, 0)),
                  pl.BlockSpec((lc, dk // 2), lambda b, c: (c, 0)),
                  full(dint), full(dq), full(dkk), full(dc)],
        out_specs=pl.BlockSpec((lc, 2 * d), lambda b, c: (row(b, c), 0)),
        out_shape=jax.ShapeDtypeStruct((n, 2 * d), BF16),
        scratch_shapes=[pltpu.VMEM((heads, dk, dv), F32)],
        compiler_params=_cparams("parallel", "arbitrary"),
        name="retention_chunk",
    )(qkvg, qkvg, qkvg, qkvg, cos, sin, dint, dq, dkk, dc)


def _mlstm_kernel(q_ref, k_ref, v_ref, og_ref, gt_ref, cw_ref, cb_ref, gb_ref, ng_ref, tril_ref,
                  o_ref, qext, kext, c_ref, n_ref, m_ref, *, lc, dqk, dv, heads):
    chunk = pl.program_id(1)
    qw = heads * dqk

    @pl.when(chunk == 0)
    def _():
        qext[0:SUBLANES, :] = jnp.zeros((SUBLANES, qw), F32)
        kext[0:SUBLANES, :] = jnp.zeros((SUBLANES, qw), F32)
        c_ref[...] = jnp.zeros_like(c_ref)
        n_ref[...] = jnp.zeros_like(n_ref)
        m_ref[...] = jnp.zeros_like(m_ref)

    def conv_silu(raw_ref, ext, c0):
        ext[SUBLANES:SUBLANES + lc, :] = raw_ref[...]
        y = cb_ref[:, c0:c0 + qw]
        for j in range(MLSTM_CONV):
            off = SUBLANES - (MLSTM_CONV - 1) + j
            y = y + cw_ref[j:j + 1, c0:c0 + qw] * ext[off:off + lc, :]
        ext[0:SUBLANES, :] = ext[lc:lc + SUBLANES, :]
        return _silu(y)

    q_all = conv_silu(q_ref, qext, 0)
    k_all = conv_silu(k_ref, kext, qw) * (dqk ** -0.5)

    gt = gt_ref[...] + gb_ref[...]
    lane = lax.broadcasted_iota(jnp.int32, gt.shape, 1)
    tril = tril_ref[...]
    causal = (lax.broadcasted_iota(jnp.int32, (lc, lc), 0)
              >= lax.broadcasted_iota(jnp.int32, (lc, lc), 1))

    hs = range(heads)
    q = [q_all[:, h * dqk:(h + 1) * dqk] for h in hs]
    k = [k_all[:, h * dqk:(h + 1) * dqk] for h in hs]
    v = [v_ref[:, h * dv:(h + 1) * dv] for h in hs]
    ipre = [jnp.sum(jnp.where(lane == h, gt, 0.0), axis=-1, keepdims=True) for h in hs]
    fpre = [jnp.sum(jnp.where(lane == h + heads, gt, 0.0), axis=-1, keepdims=True) for h in hs]
    li = [GATE_CAP * jnp.tanh(x / GATE_CAP) for x in ipre]
    lf = [-_softplus(-(GATE_CAP * jnp.tanh(x / GATE_CAP))) for x in fpre]
    b_full = [_mm_split_rhs(tril, jnp.broadcast_to(x, (lc, lc))) for x in lf]
    b_col = [x[:, 0:1] for x in b_full]
    dmat = [jnp.where(causal, b_full[h] - b_full[h].T + jnp.broadcast_to(li[h], (lc, lc)).T,
                      NEG_BIG) for h in hs]
    m_st = [m_ref[h] for h in hs]
    m_inter = [b_col[h] + m_st[h] for h in hs]
    m_t = [jnp.maximum(m_inter[h], jnp.max(dmat[h], axis=-1, keepdims=True)) for h in hs]
    scores = [_mm_nt(q[h], k[h]) * jnp.exp(dmat[h] - m_t[h]) for h in hs]
    w_inter = [jnp.exp(m_inter[h] - m_t[h]) for h in hs]
    c_st = [c_ref[h] for h in hs]
    n_st = [n_ref[h] for h in hs]
    num = [_mm(scores[h], v[h]) + w_inter[h] * _mm(q[h], c_st[h]) for h in hs]
    den = [jnp.sum(scores[h], axis=-1, keepdims=True)
           + w_inter[h] * jnp.sum(q[h] * n_st[h], axis=-1, keepdims=True) for h in hs]
    for h in hs:
        b_last = b_col[h][lc - 1:lc, :]
        gdec = b_last - b_col[h] + li[h]
        m_new = jnp.maximum(b_last + m_st[h], jnp.max(gdec, axis=0, keepdims=True))
        wk = jnp.exp(gdec - m_new)
        carry = jnp.exp(b_last + m_st[h] - m_new)
        kw = k[h] * wk
        c_ref[h] = carry * c_st[h] + _mm(kw.T, v[h])
        n_ref[h] = carry * n_st[h] + jnp.sum(kw, axis=0, keepdims=True)
        m_ref[h] = m_new
    for h in hs:
        hh = num[h] / jnp.maximum(jnp.abs(den[h]), jnp.exp(-m_t[h]))
        cs = slice(h * dv, (h + 1) * dv)
        hn = hh * lax.rsqrt(jnp.mean(hh * hh, axis=-1, keepdims=True) + EPS) * ng_ref[:, cs]
        o_ref[:, cs] = (hn * jax.nn.sigmoid(og_ref[:, cs])).astype(o_ref.dtype)


def _mlstm(proj, conv_w, conv_b, gate_b, norm_g, n_b, t_len, d):
    n = proj.shape[0]
    heads = MLSTM_HEADS
    dqk = d // (2 * heads)
    dv = d // heads
    lc = MLSTM_CHUNK
    nc = t_len // lc
    qw = heads * dqk
    gtb = (2 * qw + 2 * d) // LANES
    gb = jnp.zeros((1, LANES), F32).at[0, :2 * heads].set(gate_b.astype(F32))
    tril = jnp.asarray(np.tril(np.ones((lc, lc), np.float32)), BF16)
    kern = functools.partial(_mlstm_kernel, lc=lc, dqk=dqk, dv=dv, heads=heads)
    row = lambda b, c: b * nc + c
    full = lambda a: pl.BlockSpec(a.shape, lambda b, c: (0,) * a.ndim)
    cb = conv_b.reshape(1, 2 * qw)
    ng = norm_g.reshape(1, d)
    return pl.pallas_call(
        kern,
        grid=(n_b, nc),
        in_specs=[pl.BlockSpec((lc, qw), lambda b, c: (row(b, c), 0)),
                  pl.BlockSpec((lc, qw), lambda b, c: (row(b, c), 1)),
                  pl.BlockSpec((lc, d), lambda b, c: (row(b, c), (2 * qw) // d)),
                  pl.BlockSpec((lc, d), lambda b, c: (row(b, c), (2 * qw) // d + 1)),
                  pl.BlockSpec((lc, LANES), lambda b, c: (row(b, c), gtb)),
                  full(conv_w), full(cb), full(gb), full(ng), full(tril)],
        out_specs=pl.BlockSpec((lc, d), lambda b, c: (row(b, c), 0)),
        out_shape=jax.ShapeDtypeStruct((n, d), BF16),
        scratch_shapes=[pltpu.VMEM((lc + SUBLANES, qw), F32),
                        pltpu.VMEM((lc + SUBLANES, qw), F32),
                        pltpu.VMEM((heads, dqk, dv), F32),
                        pltpu.VMEM((heads, 1, dqk), F32),
                        pltpu.VMEM((heads, 1, 1), F32)],
        compiler_params=_cparams("parallel", "arbitrary"),
        name="mlstm_chunk",
    )(proj, proj, proj, proj, proj, conv_w, cb, gb, ng, tril)


def _rwkv_kernel(pr_ref, pk_ref, pv_ref, pl_ref, mur_ref, muk_ref, muv_ref, mul_ref,
                 w0_ref, a0_ref, kk_ref, ka_ref, lnw_ref, lnb_ref, rk_ref,
                 w2_ref, a2_ref, g2_ref, tril_ref, ms_ref, mi_ref, eye_ref,
                 o_ref, h_ref, prev_ref, ext_ref, y_ref, *, lc, nchunk, npp):
    block = pl.program_id(2)
    lb = lc * nchunk
    pw = npp * LANES

    @pl.when(block == 0)
    def _():
        h_ref[...] = jnp.zeros_like(h_ref)
        prev_ref[...] = jnp.zeros_like(prev_ref)

    lane = lax.broadcasted_iota(jnp.int32, (lb, LANES), 1)
    head0 = lane < RWKV_HEAD
    head0_c = lax.broadcasted_iota(jnp.int32, (lc, LANES), 1) < RWKV_HEAD

    def pair_sum(x):
        s0 = jnp.sum(jnp.where(head0, x, 0.0), axis=-1, keepdims=True)
        s1 = jnp.sum(jnp.where(head0, 0.0, x), axis=-1, keepdims=True)
        return jnp.where(head0, s0, s1)

    def shifted(src_ref, mu_ref, c0, pc0):
        cs = slice(c0, c0 + LANES)
        ext_ref[0:SUBLANES, :] = prev_ref[:, pc0:pc0 + LANES]
        ext_ref[SUBLANES:SUBLANES + lb, :] = src_ref[:, cs]
        cur = src_ref[:, cs]
        return cur + mu_ref[:, cs] * (ext_ref[SUBLANES - 1:SUBLANES - 1 + lb, :] - cur)

    lo0 = 3 * pw
    wlo = jnp.tanh(shifted(pl_ref, mul_ref, 0, lo0)).astype(BF16)
    alo = shifted(pl_ref, mul_ref, LANES, lo0 + LANES).astype(BF16)
    glo = jax.nn.sigmoid(jnp.concatenate(
        [shifted(pl_ref, mul_ref, 2 * LANES + s, lo0 + 2 * LANES + s)
         for s in range(0, RWKV_GATE_LORA, LANES)], axis=-1)).astype(BF16)

    r_p, k_p, v_p, kn_p, bb_p, wl_p, g_p = [], [], [], [], [], [], []
    for pi in range(npp):
        cs = slice(pi * LANES, (pi + 1) * LANES)
        r = shifted(pr_ref, mur_ref, pi * LANES, pi * LANES)
        k = shifted(pk_ref, muk_ref, pi * LANES, pw + pi * LANES)
        v = shifted(pv_ref, muv_ref, pi * LANES, 2 * pw + pi * LANES)
        wraw = w0_ref[:, cs] + jnp.dot(wlo, w2_ref[:, cs], preferred_element_type=F32)
        ag = jax.nn.sigmoid(a0_ref[:, cs] + jnp.dot(alo, a2_ref[:, cs],
                                                    preferred_element_type=F32))
        kn = k * kk_ref[:, cs]
        kn = kn / jnp.maximum(jnp.sqrt(pair_sum(kn * kn)), 1e-12)
        r_p.append(r)
        k_p.append(k * (1.0 + (ag - 1.0) * ka_ref[:, cs]))
        v_p.append(v)
        kn_p.append(kn)
        bb_p.append(kn * ag)
        wl_p.append(-jnp.exp(-_softplus(-wraw) - 0.5))
        g_p.append(jnp.dot(glo, g2_ref[:, cs], preferred_element_type=F32))

    tail = slice(lb - SUBLANES, lb)
    prev_ref[:, 0:pw] = pr_ref[tail, :]
    prev_ref[:, pw:2 * pw] = pk_ref[tail, :]
    prev_ref[:, 2 * pw:3 * pw] = pv_ref[tail, :]
    prev_ref[:, 3 * pw:3 * pw + RWKV_LORA_W] = pl_ref[tail, :]

    def stack(x):
        return jnp.concatenate([jnp.where(head0_c, x, 0.0), jnp.where(head0_c, 0.0, x)], axis=0)

    tril = tril_ref[...]
    ms = ms_ref[...]
    mi = mi_ref[...]
    eye = eye_ref[...]
    two = 2 * lc
    probs = [(pi, ci) for ci in range(nchunk) for pi in range(npp)]
    ps = range(len(probs))

    def rows(vals):
        return [vals[pi][ci * lc:(ci + 1) * lc, :] for pi, ci in probs]

    wl = rows(wl_p)
    cum = [_mm_split_rhs(tril, x) for x in wl]
    cl = [x[lc - 1:lc, :] for x in cum]
    r, k, v, kn, bb = rows(r_p), rows(k_p), rows(v_p), rows(kn_p), rows(bb_p)
    e_n = [jnp.exp(-cum[i]) for i in ps]
    e_l = [jnp.exp(cl[i] - cum[i]) for i in ps]
    a_s = [stack(-kn[i] * jnp.exp(cum[i] - wl[i])) for i in ps]
    r_s = [stack(r[i] * jnp.exp(cum[i])) for i in ps]
    b_s = [stack(bb[i] * e_n[i]) for i in ps]
    k_s = [stack(k[i] * e_n[i]) for i in ps]
    v_s = [stack(v[i]) for i in ps]
    bp_t = [stack(bb[i] * e_l[i]).T for i in ps]
    kp_t = [stack(k[i] * e_l[i]).T for i in ps]

    gm = [_mm_nt(jnp.concatenate([a_s[i], r_s[i]], axis=0),
                 jnp.concatenate([b_s[i], k_s[i]], axis=0)) for i in ps]
    npow = [gm[i][:two, :two] * ms for i in ps]
    aak_v = [_mm(gm[i][:two, two:] * ms, v_s[i]) for i in ps]
    arb = [gm[i][two:, :two] * mi for i in ps]
    ark_v = [_mm(gm[i][two:, two:] * mi, v_s[i]) for i in ps]
    kp_v = [_mm(kp_t[i], v_s[i]) for i in ps]

    tinv = [eye + npow[i] for i in ps]
    for _ in range(int(np.log2(lc)) - 1):
        npow = [_mm(npow[i], npow[i]) for i in ps]
        tinv = [_mm(tinv[i], eye + npow[i]) for i in ps]

    taw = [_mm(tinv[i], jnp.concatenate([a_s[i], aak_v[i]], axis=1)) for i in ps]
    arb_taw = [_mm(arb[i], taw[i]) for i in ps]
    bp_taw = [_mm(bp_t[i], taw[i]) for i in ps]
    qh = [r_s[i] + arb_taw[i][:, :LANES] for i in ps]
    yin = [arb_taw[i][:, LANES:] + ark_v[i] for i in ps]
    phi = [eye * jnp.exp(cl[i]) + bp_taw[i][:, :LANES] for i in ps]
    psi = [bp_taw[i][:, LANES:] + kp_v[i] for i in ps]

    hst = [h_ref[pi] for pi in range(npp)]
    for i, (pi, ci) in enumerate(probs):
        ys = _mm(qh[i], hst[pi]) + yin[i]
        hst[pi] = _mm(phi[i], hst[pi]) + psi[i]
        y_ref[pi, ci * lc:(ci + 1) * lc, :] = ys[:lc] + ys[lc:]
    for pi in range(npp):
        h_ref[pi] = hst[pi]

    inv_head = 1.0 / RWKV_HEAD
    for pi in range(npp):
        cs = slice(pi * LANES, (pi + 1) * LANES)
        y = y_ref[pi]
        yc = y - pair_sum(y) * inv_head
        var = pair_sum(yc * yc) * inv_head
        yn = yc * lax.rsqrt(var + RWKV_LN_EPS) * lnw_ref[:, cs] + lnb_ref[:, cs]
        bonus = pair_sum(r_p[pi] * k_p[pi] * rk_ref[:, cs])
        o_ref[:, cs] = ((yn + bonus * v_p[pi]) * g_p[pi]).astype(o_ref.dtype)


def _rwkv_mix(p, mu, w0, a0, k_k, k_a, ln_w, ln_b, r_k, w2, a2, g2, n_b, t_len, d,
              nchunk=4, npp=4):
    lc = RWKV_CHUNK
    lb = lc * nchunk
    nb = t_len // lb
    pw = npp * LANES
    ngrp = d // pw
    two = 2 * lc
    lora_blk = (3 * d) // RWKV_LORA_W
    idx = np.arange(two)
    same = (idx[:, None] // lc) == (idx[None, :] // lc)
    ms = jnp.asarray((same & (idx[:, None] > idx[None, :])).astype(np.float32))
    mi = jnp.asarray((same & (idx[:, None] >= idx[None, :])).astype(np.float32))
    eye = jnp.asarray(np.eye(two, dtype=np.float32))
    tril = jnp.asarray(np.tril(np.ones((lc, lc), np.float32)), BF16)
    row = lambda b, t: b * nb + t
    pspec = lambda off: pl.BlockSpec((lb, pw), lambda b, g, t: (row(b, t), off + g))
    mspec = lambda off: pl.BlockSpec((1, pw), lambda b, g, t: (0, off + g))
    vspec = pl.BlockSpec((1, pw), lambda b, g, t: (0, g))
    wspec = lambda rows: pl.BlockSpec((rows, pw), lambda b, g, t: (0, g))
    sq = lambda s: pl.BlockSpec((s, s), lambda b, g, t: (0, 0))
    vec = lambda a: a.reshape(1, -1)
    mu2 = vec(mu)
    kern = functools.partial(_rwkv_kernel, lc=lc, nchunk=nchunk, npp=npp)
    return pl.pallas_call(
        kern,
        grid=(n_b, ngrp, nb),
        in_specs=[pspec(0), pspec(ngrp), pspec(2 * ngrp),
                  pl.BlockSpec((lb, RWKV_LORA_W), lambda b, g, t: (row(b, t), lora_blk)),
                  mspec(0), mspec(ngrp), mspec(2 * ngrp),
                  pl.BlockSpec((1, RWKV_LORA_W), lambda b, g, t: (0, lora_blk)),
                  vspec, vspec, vspec, vspec, vspec, vspec, vspec,
                  wspec(LANES), wspec(LANES), wspec(RWKV_GATE_LORA),
                  sq(lc), sq(two), sq(two), sq(two)],
        out_specs=pl.BlockSpec((lb, pw), lambda b, g, t: (row(b, t), g)),
        out_shape=jax.ShapeDtypeStruct((n_b * t_len, d), BF16),
        scratch_shapes=[pltpu.VMEM((npp, LANES, LANES), F32),
                        pltpu.VMEM((SUBLANES, 3 * pw + RWKV_LORA_W), F32),
                        pltpu.VMEM((lb + SUBLANES, LANES), F32),
                        pltpu.VMEM((npp, lb, LANES), F32)],
        compiler_params=_cparams("parallel", "parallel", "arbitrary"),
        name="rwkv_mix",
    )(p, p, p, p, mu2, mu2, mu2, mu2, vec(w0), vec(a0), vec(k_k), vec(k_a), vec(ln_w),
      vec(ln_b), vec(r_k), w2, a2, g2, tril, ms, mi, eye)


def _final_kernel(x_ref, g_ref, sh_ref, sc_ref, o_ref):
    o_ref[...] = _rms_mod(x_ref[...], g_ref[...], sh_ref[0], sc_ref[0])


def _final_norm(x2, g, shift, scale, t_len, tm=256):
    n, d = x2.shape
    tm = min(tm, t_len)
    tpb = t_len // tm
    return pl.pallas_call(
        _final_kernel,
        grid=(n // tm,),
        in_specs=[pl.BlockSpec((tm, d), lambda i: (i, 0)),
                  pl.BlockSpec((1, d), lambda i: (0, 0)),
                  pl.BlockSpec((1, 1, d), lambda i: (i // tpb, 0, 0)),
                  pl.BlockSpec((1, 1, d), lambda i: (i // tpb, 0, 0))],
        out_specs=pl.BlockSpec((tm, d), lambda i: (i, 0)),
        out_shape=jax.ShapeDtypeStruct((n, d), F32),
        compiler_params=_cparams("parallel"),
        name="final_norm",
    )(x2, g.reshape(1, d), shift, scale)


def _pad_cols(w, width):
    return jnp.pad(w, ((0, 0), (0, width - w.shape[1])))


def _rwkv_pack_in(w_in, mu, d):
    c0 = 3 * d
    c1 = c0 + RWKV_DECAY_LORA
    c2 = c1 + RWKV_A_LORA
    def pack(a):
        return jnp.concatenate([a[..., :c0], _pad_cols(a[..., c0:c1], LANES),
                                _pad_cols(a[..., c1:c2], LANES), a[..., c2:]], axis=-1)
    return pack(w_in), pack(mu.reshape(1, -1))[0]


def _pad_rows(w, rows):
    return jnp.pad(w, ((0, rows - w.shape[0]), (0, 0)))


def kernel(x, c, mod_w, mod_b, norm_mix_g, norm_ffn_g, ret_w_in, ret_w_out, rwkv_w_in, rwkv_mu, rwkv_w0, rwkv_w2, rwkv_a0, rwkv_a2, rwkv_g2, rwkv_k_k, rwkv_k_a, rwkv_r_k, rwkv_ln_w, rwkv_ln_b, rwkv_w_out, mlstm_w_in, mlstm_conv_w, mlstm_conv_b, mlstm_gate_b, mlstm_norm_g, mlstm_w_out, ffn_w_up, ffn_conv_w, ffn_conv_b, ffn_w_down, final_g, final_mod_w, final_mod_b):
    n_b, t_len, d = x.shape
    depth = mod_w.shape[0]
    x2 = x.reshape(n_b * t_len, d)

    c_pad = jnp.pad(c, ((0, SUBLANES - n_b), (0, 0)))
    mod = _mod_vectors(c_pad, mod_w, mod_b)[:, :n_b]
    fmod = _mod_vectors(c_pad, final_mod_w[None], final_mod_b[None])[0, :n_b]

    def mvec(i, j):
        return mod[i, :, j * d:(j + 1) * d][:, None, :]

    for i in range(depth):
        sh_m, sc_m, g_m, sh_f, sc_f, g_f = (mvec(i, j) for j in range(6))
        kind, j = i % N_MIXERS, i // N_MIXERS
        if kind == 0:
            qkvg = _nm_linear(x2, norm_mix_g[i], sh_m, sc_m, ret_w_in, j, t_len, BF16)
            o = _retention(qkvg, n_b, t_len, d)
            x2 = _linear_res(o, ret_w_out[j].astype(BF16), x2, g_m, t_len)
        elif kind == 1:
            w_in, mu = _rwkv_pack_in(rwkv_w_in[j], rwkv_mu[j], d)
            p = _nm_linear(x2, norm_mix_g[i], sh_m, sc_m, w_in[None], 0, t_len, F32)
            o = _rwkv_mix(p, mu, rwkv_w0[j], rwkv_a0[j], rwkv_k_k[j], rwkv_k_a[j], rwkv_ln_w[j],
                          rwkv_ln_b[j], rwkv_r_k[j],
                          _pad_rows(rwkv_w2[j], LANES).astype(BF16),
                          _pad_rows(rwkv_a2[j], LANES).astype(BF16),
                          rwkv_g2[j].astype(BF16), n_b, t_len, d)
            x2 = _linear_res(o, rwkv_w_out[j].astype(BF16), x2, g_m, t_len)
        else:
            w_in = mlstm_w_in[j]
            w_in = _pad_cols(w_in, -(-w_in.shape[1] // 512) * 512)
            proj = _nm_linear(x2, norm_mix_g[i], sh_m, sc_m, w_in[None], 0, t_len, F32)
            o = _mlstm(proj, mlstm_conv_w[j], mlstm_conv_b[j], mlstm_gate_b[j], mlstm_norm_g[j],
                       n_b, t_len, d)
            x2 = _linear_res(o, mlstm_w_out[j].astype(BF16), x2, g_m, t_len)
        act = _ffn_up(x2, norm_ffn_g[i], sh_f, sc_f, ffn_w_up, i, ffn_conv_w[i], ffn_conv_b[i],
                      t_len)
        x2 = _linear_res(act, ffn_w_down[i].astype(BF16), x2, g_f, t_len)

    out = _final_norm(x2, final_g, fmod[:, :d][:, None, :], fmod[:, d:][:, None, :], t_len)
    return out.reshape(n_b, t_len, d)
```

```python
import functools

import numpy as np
import jax
import jax.numpy as jnp
from jax import lax
from jax.experimental import pallas as pl
from jax.experimental.pallas import tpu as pltpu

F32 = jnp.float32
BF16 = jnp.bfloat16

V7X_VMEM_BYTES = 64 * 1024 * 1024
VMEM_LIMIT = (V7X_VMEM_BYTES * 7) // 8
LANES = 128
SUBLANES = 8
BF16_ROWS = 16
NORM_ROWS = 16

EPS = 1e-6
N_MIXERS = 3

RET_HEADS = 8
ROPE_BASE = 10000.0
RET_CHUNK = 128

RWKV_HEAD = 64
RWKV_DECAY_LORA = 96
RWKV_A_LORA = 96
RWKV_GATE_LORA = 256
RWKV_LORA_W = 2 * LANES + RWKV_GATE_LORA
RWKV_LN_EPS = 64e-5
RWKV_CHUNK = 64

MLSTM_HEADS = 4
MLSTM_CONV = 4
MLSTM_CHUNK = 128
GATE_CAP = 15.0

FFN_CONV = 3
NEG_BIG = -1e30


def _cparams(*sem):
    return pltpu.CompilerParams(dimension_semantics=sem, vmem_limit_bytes=VMEM_LIMIT)


def _mm(a, b):
    return jnp.dot(a.astype(BF16), b.astype(BF16), preferred_element_type=F32)


def _mm_nt(a, b):
    return lax.dot_general(a.astype(BF16), b.astype(BF16), (((1,), (1,)), ((), ())),
                           preferred_element_type=F32)


def _split(x):
    hi = x.astype(BF16)
    lo = (x - hi.astype(F32)).astype(BF16)
    return hi, lo


def _mm_split_rhs(m, x):
    hi, lo = _split(x)
    return (jnp.dot(m, hi, preferred_element_type=F32)
            + jnp.dot(m, lo, preferred_element_type=F32))


def _silu(x):
    return x * jax.nn.sigmoid(x)


def _softplus(z):
    return jnp.maximum(z, 0.0) + jnp.log(1.0 + jnp.exp(-jnp.abs(z)))


def _rms_mod(x, g, shift, scale):
    y = x * lax.rsqrt(jnp.mean(x * x, axis=-1, keepdims=True) + EPS)
    return (y * g) * (1.0 + scale) + shift


def _mod_kernel(c_ref, w_ref, b_ref, o_ref):
    ca = _silu(c_ref[...])
    o_ref[0] = jnp.dot(ca, w_ref[0], precision=lax.Precision.HIGHEST,
                       preferred_element_type=F32) + b_ref[0]


def _mod_vectors(c_pad, w, b, tn=1024):
    depth, d, n = w.shape
    rows = c_pad.shape[0]
    return pl.pallas_call(
        _mod_kernel,
        grid=(depth, n // tn),
        in_specs=[pl.BlockSpec((rows, d), lambda i, j: (0, 0)),
                  pl.BlockSpec((1, d, tn), lambda i, j: (i, 0, j)),
                  pl.BlockSpec((1, 1, tn), lambda i, j: (i, 0, j))],
        out_specs=pl.BlockSpec((1, rows, tn), lambda i, j: (i, 0, j)),
        out_shape=jax.ShapeDtypeStruct((depth, rows, n), F32),
        compiler_params=_cparams("parallel", "parallel"),
        name="mod_vectors",
    )(c_pad, w, b.reshape(depth, 1, n))


def _mod_spec(d, layer, which, tpb, tn=None):
    if tn is None:
        return pl.BlockSpec((None, None, None, 1, d),
                            lambda i, *_: (layer, i // tpb, which, 0, 0))
    return pl.BlockSpec((None, None, None, 1, tn), lambda i, j: (layer, i // tpb, which, 0, j))


def _nm_linear_kernel(x_ref, g_ref, sh_ref, sc_ref, w_ref, *rest, tm, rchunk, has_tail):
    wt_ref = rest[0] if has_tail else None
    o_ref, h_ref = rest[-2:]
    col_tile = pl.program_id(1)
    last = pl.num_programs(1) - 1

    @pl.when(col_tile == 0)
    def _():
        for r0 in range(0, tm, rchunk):
            h = _rms_mod(x_ref[r0:r0 + rchunk, :], g_ref[...], sh_ref[...], sc_ref[...])
            h_ref[r0:r0 + rchunk, :] = h.astype(BF16)

    def project(wref):
        o_ref[...] = jnp.dot(h_ref[...], wref[...].astype(BF16),
                             preferred_element_type=F32).astype(o_ref.dtype)

    if has_tail:
        pl.when(col_tile < last)(lambda: project(w_ref))
        pl.when(col_tile == last)(lambda: project(wt_ref))
    else:
        project(w_ref)


def _nm_linear(x2, g, mod5, mlayer, sh_idx, w, wlayer, t_len, out_dtype, w_tail=None, tm=1024):
    n, d = x2.shape
    tn = 1024 if (w_tail is None and w.shape[2] % 1024 == 0) else 512
    nmain = w.shape[2] // tn
    ncol = nmain + (0 if w_tail is None else w_tail.shape[1] // tn)
    tm = min(tm, t_len)
    tpb = t_len // tm
    kern = functools.partial(_nm_linear_kernel, tm=tm, rchunk=min(NORM_ROWS, tm),
                             has_tail=w_tail is not None)
    in_specs = [pl.BlockSpec((tm, d), lambda i, j: (i, 0)),
                pl.BlockSpec((1, d), lambda i, j: (0, 0)),
                _mod_spec(d, mlayer, sh_idx, tpb),
                _mod_spec(d, mlayer, sh_idx + 1, tpb),
                pl.BlockSpec((None, d, tn), lambda i, j: (wlayer, 0, jnp.minimum(j, nmain - 1)))]
    args = [x2, g.reshape(1, d), mod5, mod5, w]
    if w_tail is not None:
        in_specs.append(pl.BlockSpec((d, tn), lambda i, j: (0, jnp.maximum(j - nmain, 0))))
        args.append(w_tail)
    return pl.pallas_call(
        kern,
        grid=(n // tm, ncol),
        in_specs=in_specs,
        out_specs=pl.BlockSpec((tm, tn), lambda i, j: (i, j)),
        out_shape=jax.ShapeDtypeStruct((n, ncol * tn), out_dtype),
        scratch_shapes=[pltpu.VMEM((tm, d), BF16)],
        compiler_params=_cparams("parallel", "arbitrary"),
        name="norm_mod_linear",
    )(*args)


def _ffn_up_kernel(x_ref, xp_ref, g_ref, sh_ref, sc_ref, wg_ref, wv_ref, cw_ref, cb_ref,
                   o_ref, h_ref, ext_ref, *, tm, rchunk, tpb):
    first = (pl.program_id(0) % tpb) == 0
    col_tile = pl.program_id(1)
    halo = BF16_ROWS

    @pl.when(col_tile == 0)
    def _():
        hp = _rms_mod(xp_ref[...], g_ref[...], sh_ref[...], sc_ref[...])
        h_ref[0:halo, :] = hp.astype(BF16)
        for r0 in range(0, tm, NORM_ROWS):
            h = _rms_mod(x_ref[r0:r0 + NORM_ROWS, :], g_ref[...], sh_ref[...], sc_ref[...])
            h_ref[halo + r0:halo + r0 + NORM_ROWS, :] = h.astype(BF16)

    ext_ref[...] = jnp.dot(h_ref[...], wg_ref[...].astype(BF16), preferred_element_type=F32)
    ext_ref[0:halo, :] = jnp.where(first, 0.0, ext_ref[0:halo, :])
    val = jnp.dot(h_ref[halo:halo + tm, :], wv_ref[...].astype(BF16),
                  preferred_element_type=F32)
    for r0 in range(0, tm, rchunk):
        base = halo + r0
        conv = (cw_ref[0:1, :] * ext_ref[base - 2:base - 2 + rchunk, :]
                + cw_ref[1:2, :] * ext_ref[base - 1:base - 1 + rchunk, :]
                + cw_ref[2:3, :] * ext_ref[base:base + rchunk, :]
                + cb_ref[...])
        o_ref[r0:r0 + rchunk, :] = (_silu(conv) * val[r0:r0 + rchunk, :]).astype(o_ref.dtype)


def _ffn_up(x2, g, mod5, sh_idx, w_up, conv_w, conv_b, layer, t_len, tm=1024, tn=512):
    n, d = x2.shape
    f = w_up.shape[2] // 2
    tm = min(tm, t_len)
    tpb = t_len // tm
    rb = tm // BF16_ROWS
    nf = f // tn
    kern = functools.partial(_ffn_up_kernel, tm=tm, rchunk=min(256, tm), tpb=tpb)
    return pl.pallas_call(
        kern,
        grid=(n // tm, nf),
        in_specs=[pl.BlockSpec((tm, d), lambda i, j: (i, 0)),
                  pl.BlockSpec((BF16_ROWS, d), lambda i, j: (jnp.maximum(i * rb - 1, 0), 0)),
                  pl.BlockSpec((1, d), lambda i, j: (0, 0)),
                  _mod_spec(d, layer, sh_idx, tpb),
                  _mod_spec(d, layer, sh_idx + 1, tpb),
                  pl.BlockSpec((None, d, tn), lambda i, j: (layer, 0, j)),
                  pl.BlockSpec((None, d, tn), lambda i, j: (layer, 0, nf + j)),
                  pl.BlockSpec((None, FFN_CONV, tn), lambda i, j: (layer, 0, j)),
                  pl.BlockSpec((None, 1, tn), lambda i, j: (layer, 0, j))],
        out_specs=pl.BlockSpec((tm, tn), lambda i, j: (i, j)),
        out_shape=jax.ShapeDtypeStruct((n, f), BF16),
        scratch_shapes=[pltpu.VMEM((tm + BF16_ROWS, d), BF16),
                        pltpu.VMEM((tm + BF16_ROWS, tn), F32)],
        compiler_params=_cparams("parallel", "arbitrary"),
        name="ffn_up_act",
    )(x2, x2, g.reshape(1, d), mod5, mod5, w_up, w_up, conv_w,
      conv_b.reshape(conv_b.shape[0], 1, f))


def _linear_res_kernel(a_ref, w_ref, x_ref, gate_ref, o_ref):
    acc = jnp.dot(a_ref[...], w_ref[...], preferred_element_type=F32)
    o_ref[...] = x_ref[...] + gate_ref[...] * acc


def _linear_res(a, w, wlayer, x2, mod5, mlayer, gate_idx, t_len, tm=1024):
    n, k = a.shape
    m = w.shape[2]
    tn = 1024 if k <= 2048 else 512
    tm = min(tm, t_len)
    tpb = t_len // tm
    return pl.pallas_call(
        _linear_res_kernel,
        grid=(n // tm, m // tn),
        in_specs=[pl.BlockSpec((tm, k), lambda i, j: (i, 0)),
                  pl.BlockSpec((None, k, tn), lambda i, j: (wlayer, 0, j)),
                  pl.BlockSpec((tm, tn), lambda i, j: (i, j)),
                  _mod_spec(m, mlayer, gate_idx, tpb, tn)],
        out_specs=pl.BlockSpec((tm, tn), lambda i, j: (i, j)),
        out_shape=jax.ShapeDtypeStruct((n, m), F32),
        compiler_params=_cparams("parallel", "arbitrary"),
        name="linear_residual",
    )(a, w, x2, mod5)


def _retention_kernel(q_ref, k_ref, v_ref, g_ref, cos_ref, sin_ref, dint_ref, dq_ref, dk_ref,
                      dc_ref, o_ref, s_ref, *, heads, dk, dv):
    chunk = pl.program_id(1)

    @pl.when(chunk == 0)
    def _():
        s_ref[...] = jnp.zeros_like(s_ref)

    cos = cos_ref[...]
    sin = sin_ref[...]
    half = dk // 2

    def rot(ref, h):
        x1 = ref[:, h * dk:h * dk + half].astype(F32)
        x2 = ref[:, h * dk + half:(h + 1) * dk].astype(F32)
        return jnp.concatenate([x1 * cos - x2 * sin, x1 * sin + x2 * cos], axis=-1)

    hs = range(heads)
    q = [rot(q_ref, h) for h in hs]
    k = [rot(k_ref, h) * (dk ** -0.5) for h in hs]
    v = [v_ref[:, h * dv:(h + 1) * dv] for h in hs]
    scores = [_mm_nt(q[h], k[h]) * dint_ref[h] for h in hs]
    state = [s_ref[h] for h in hs]
    out = [_mm(scores[h], v[h]) + _mm(q[h] * dq_ref[h], state[h]) for h in hs]
    for h in hs:
        s_ref[h] = dc_ref[h] * state[h] + _mm((k[h] * dk_ref[h]).T, v[h])
    for h in hs:
        oc = out[h] - jnp.mean(out[h], axis=-1, keepdims=True)
        o = oc * lax.rsqrt(jnp.mean(oc * oc, axis=-1, keepdims=True) + EPS)
        gate = g_ref[:, h * dv:(h + 1) * dv].astype(F32)
        o_ref[:, h * dv:(h + 1) * dv] = (o * _silu(gate)).astype(o_ref.dtype)


def _retention_tables(t_len, dk):
    half = dk // 2
    inv = ROPE_BASE ** (-np.arange(half, dtype=np.float64) / half)
    ang = np.arange(t_len, dtype=np.float64)[:, None] * inv[None, :]
    lg = np.log1p(-np.power(2.0, -5.0 - np.arange(RET_HEADS, dtype=np.float64)))
    idx = np.arange(RET_CHUNK, dtype=np.float64)
    diff = idx[:, None] - idx[None, :]
    dint = np.where(diff >= 0, np.exp(lg[:, None, None] * np.maximum(diff, 0.0)), 0.0)
    dq = np.exp(lg[:, None] * (idx + 1.0))[..., None]
    dkk = np.exp(lg[:, None] * (RET_CHUNK - 1.0 - idx))[..., None]
    dc = np.exp(lg * RET_CHUNK)[:, None, None]
    f = lambda a: jnp.asarray(a, F32)
    return f(np.cos(ang)), f(np.sin(ang)), f(dint), f(dq), f(dkk), f(dc)


def _retention(qkvg, n_b, t_len, d):
    n = qkvg.shape[0]
    heads = RET_HEADS
    dk = d // heads
    dv = 2 * dk
    lc = RET_CHUNK
    nc = t_len // lc
    cos, sin, dint, dq, dkk, dc = _retention_tables(t_len, dk)
    row = lambda b, c: b * nc + c
    full = lambda a: pl.BlockSpec(a.shape, lambda b, c: (0,) * a.ndim)
    kern = functools.partial(_retention_kernel, heads=heads, dk=dk, dv=dv)
    return pl.pallas_call(
        kern,
        grid=(n_b, nc),
        in_specs=[pl.BlockSpec((lc, d), lambda b, c: (row(b, c), 0)),
                  pl.BlockSpec((lc, d), lambda b, c: (row(b, c), 1)),
                  pl.BlockSpec((lc, 2 * d), lambda b, c: (row(b, c), 1)),
                  pl.BlockSpec((lc, 2 * d), lambda b, c: (row(b, c), 2)),
                  pl.BlockSpec((lc, dk // 2), lambda b, c: (c, 0)),
                  pl.BlockSpec((lc, dk // 2), lambda b, c: (c, 0)),
                  full(dint), full(dq), full(dkk), full(dc)],
        out_specs=pl.BlockSpec((lc, 2 * d), lambda b, c: (row(b, c), 0)),
        out_shape=jax.ShapeDtypeStruct((n, 2 * d), BF16),
        scratch_shapes=[pltpu.VMEM((heads, dk, dv), F32)],
        compiler_params=_cparams("parallel", "arbitrary"),
        name="retention_chunk",
    )(qkvg, qkvg, qkvg, qkvg, cos, sin, dint, dq, dkk, dc)


def _mlstm_kernel(q_ref, k_ref, v_ref, og_ref, gt_ref, cw_ref, cb_ref, gb_ref, ng_ref, tril_ref,
                  o_ref, qext, kext, c_ref, n_ref, m_ref, *, lc, dqk, dv, heads):
    chunk = pl.program_id(1)
    qw = heads * dqk

    @pl.when(chunk == 0)
    def _():
        qext[0:SUBLANES, :] = jnp.zeros((SUBLANES, qw), F32)
        kext[0:SUBLANES, :] = jnp.zeros((SUBLANES, qw), F32)
        c_ref[...] = jnp.zeros_like(c_ref)
        n_ref[...] = jnp.zeros_like(n_ref)
        m_ref[...] = jnp.zeros_like(m_ref)

    def conv_silu(raw_ref, ext, c0):
        ext[SUBLANES:SUBLANES + lc, :] = raw_ref[...]
        y = cb_ref[:, c0:c0 + qw]
        for j in range(MLSTM_CONV):
            off = SUBLANES - (MLSTM_CONV - 1) + j
            y = y + cw_ref[j:j + 1, c0:c0 + qw] * ext[off:off + lc, :]
        ext[0:SUBLANES, :] = ext[lc:lc + SUBLANES, :]
        return _silu(y)

    q_all = conv_silu(q_ref, qext, 0)
    k_all = conv_silu(k_ref, kext, qw) * (dqk ** -0.5)

    gt = gt_ref[...] + gb_ref[...]
    lane = lax.broadcasted_iota(jnp.int32, gt.shape, 1)
    tril = tril_ref[...]
    causal = (lax.broadcasted_iota(jnp.int32, (lc, lc), 0)
              >= lax.broadcasted_iota(jnp.int32, (lc, lc), 1))

    hs = range(heads)
    q = [q_all[:, h * dqk:(h + 1) * dqk] for h in hs]
    k = [k_all[:, h * dqk:(h + 1) * dqk] for h in hs]
    v = [v_ref[:, h * dv:(h + 1) * dv] for h in hs]
    ipre = [jnp.sum(jnp.where(lane == h, gt, 0.0), axis=-1, keepdims=True) for h in hs]
    fpre = [jnp.sum(jnp.where(lane == h + heads, gt, 0.0), axis=-1, keepdims=True) for h in hs]
    li = [GATE_CAP * jnp.tanh(x / GATE_CAP) for x in ipre]
    lf = [-_softplus(-(GATE_CAP * jnp.tanh(x / GATE_CAP))) for x in fpre]
    b_full = [_mm_split_rhs(tril, jnp.broadcast_to(x, (lc, lc))) for x in lf]
    b_col = [x[:, 0:1] for x in b_full]
    dmat = [jnp.where(causal, b_full[h] - b_full[h].T + jnp.broadcast_to(li[h], (lc, lc)).T,
                      NEG_BIG) for h in hs]
    m_st = [m_ref[h] for h in hs]
    m_inter = [b_col[h] + m_st[h] for h in hs]
    m_t = [jnp.maximum(m_inter[h], jnp.max(dmat[h], axis=-1, keepdims=True)) for h in hs]
    scores = [_mm_nt(q[h], k[h]) * jnp.exp(dmat[h] - m_t[h]) for h in hs]
    w_inter = [jnp.exp(m_inter[h] - m_t[h]) for h in hs]
    c_st = [c_ref[h] for h in hs]
    n_st = [n_ref[h] for h in hs]
    num = [_mm(scores[h], v[h]) + w_inter[h] * _mm(q[h], c_st[h]) for h in hs]
    den = [jnp.sum(scores[h], axis=-1, keepdims=True)
           + w_inter[h] * jnp.sum(q[h] * n_st[h], axis=-1, keepdims=True) for h in hs]
    for h in hs:
        b_last = b_col[h][lc - 1:lc, :]
        gdec = b_last - b_col[h] + li[h]
        m_new = jnp.maximum(b_last + m_st[h], jnp.max(gdec, axis=0, keepdims=True))
        wk = jnp.exp(gdec - m_new)
        carry = jnp.exp(b_last + m_st[h] - m_new)
        kw = k[h] * wk
        c_ref[h] = carry * c_st[h] + _mm(kw.T, v[h])
        n_ref[h] = carry * n_st[h] + jnp.sum(kw, axis=0, keepdims=True)
        m_ref[h] = m_new
    for h in hs:
        hh = num[h] / jnp.maximum(jnp.abs(den[h]), jnp.exp(-m_t[h]))
        cs = slice(h * dv, (h + 1) * dv)
        hn = hh * lax.rsqrt(jnp.mean(hh * hh, axis=-1, keepdims=True) + EPS) * ng_ref[:, cs]
        o_ref[:, cs] = (hn * jax.nn.sigmoid(og_ref[:, cs])).astype(o_ref.dtype)


def _mlstm(proj, conv_w, conv_b, gate_b, norm_g, n_b, t_len, d):
    n = proj.shape[0]
    heads = MLSTM_HEADS
    dqk = d // (2 * heads)
    dv = d // heads
    lc = MLSTM_CHUNK
    nc = t_len // lc
    qw = heads * dqk
    gtb = (2 * qw + 2 * d) // LANES
    gb = jnp.zeros((1, LANES), F32).at[0, :2 * heads].set(gate_b.astype(F32))
    tril = jnp.asarray(np.tril(np.ones((lc, lc), np.float32)), BF16)
    kern = functools.partial(_mlstm_kernel, lc=lc, dqk=dqk, dv=dv, heads=heads)
    row = lambda b, c: b * nc + c
    full = lambda a: pl.BlockSpec(a.shape, lambda b, c: (0,) * a.ndim)
    cb = conv_b.reshape(1, 2 * qw)
    ng = norm_g.reshape(1, d)
    return pl.pallas_call(
        kern,
        grid=(n_b, nc),
        in_specs=[pl.BlockSpec((lc, qw), lambda b, c: (row(b, c), 0)),
                  pl.BlockSpec((lc, qw), lambda b, c: (row(b, c), 1)),
                  pl.BlockSpec((lc, d), lambda b, c: (row(b, c), (2 * qw) // d)),
                  pl.BlockSpec((lc, d), lambda b, c: (row(b, c), (2 * qw) // d + 1)),
                  pl.BlockSpec((lc, LANES), lambda b, c: (row(b, c), gtb)),
                  full(conv_w), full(cb), full(gb), full(ng), full(tril)],
        out_specs=pl.BlockSpec((lc, d), lambda b, c: (row(b, c), 0)),
        out_shape=jax.ShapeDtypeStruct((n, d), BF16),
        scratch_shapes=[pltpu.VMEM((lc + SUBLANES, qw), F32),
                        pltpu.VMEM((lc + SUBLANES, qw), F32),
                        pltpu.VMEM((heads, dqk, dv), F32),
                        pltpu.VMEM((heads, 1, dqk), F32),
                        pltpu.VMEM((heads, 1, 1), F32)],
        compiler_params=_cparams("parallel", "arbitrary"),
        name="mlstm_chunk",
    )(proj, proj, proj, proj, proj, conv_w, cb, gb, ng, tril)


def _rwkv_kernel(pr_ref, pk_ref, pv_ref, pl_ref, mur_ref, muk_ref, muv_ref, mul_ref,
                 w0_ref, a0_ref, kk_ref, ka_ref, lnw_ref, lnb_ref, rk_ref,
                 w2_ref, a2_ref, g2_ref, tril_ref, ms_ref, mi_ref, eye_ref,
                 o_ref, h_ref, prev_ref, ext_ref, y_ref, *, lc, nchunk, npp):
    block = pl.program_id(2)
    lb = lc * nchunk
    pw = npp * LANES

    @pl.when(block == 0)
    def _():
        h_ref[...] = jnp.zeros_like(h_ref)
        prev_ref[...] = jnp.zeros_like(prev_ref)

    lane = lax.broadcasted_iota(jnp.int32, (lb, LANES), 1)
    head0 = lane < RWKV_HEAD
    head0_c = lax.broadcasted_iota(jnp.int32, (lc, LANES), 1) < RWKV_HEAD

    def pair_sum(x):
        s0 = jnp.sum(jnp.where(head0, x, 0.0), axis=-1, keepdims=True)
        s1 = jnp.sum(jnp.where(head0, 0.0, x), axis=-1, keepdims=True)
        return jnp.where(head0, s0, s1)

    def shifted(src_ref, mu_ref, c0, pc0):
        cs = slice(c0, c0 + LANES)
        ext_ref[0:SUBLANES, :] = prev_ref[:, pc0:pc0 + LANES]
        ext_ref[SUBLANES:SUBLANES + lb, :] = src_ref[:, cs]
        cur = src_ref[:, cs]
        return cur + mu_ref[:, cs] * (ext_ref[SUBLANES - 1:SUBLANES - 1 + lb, :] - cur)

    lo0 = 3 * pw
    wlo = jnp.tanh(shifted(pl_ref, mul_ref, 0, lo0)).astype(BF16)
    alo = shifted(pl_ref, mul_ref, LANES, lo0 + LANES).astype(BF16)
    glo = jax.nn.sigmoid(jnp.concatenate(
        [shifted(pl_ref, mul_ref, 2 * LANES + s, lo0 + 2 * LANES + s)
         for s in range(0, RWKV_GATE_LORA, LANES)], axis=-1)).astype(BF16)

    r_p, k_p, v_p, kn_p, bb_p, wl_p, g_p = [], [], [], [], [], [], []
    for pi in range(npp):
        cs = slice(pi * LANES, (pi + 1) * LANES)
        r = shifted(pr_ref, mur_ref, pi * LANES, pi * LANES)
        k = shifted(pk_ref, muk_ref, pi * LANES, pw + pi * LANES)
        v = shifted(pv_ref, muv_ref, pi * LANES, 2 * pw + pi * LANES)
        wraw = w0_ref[:, cs] + jnp.dot(wlo, w2_ref[:, cs], preferred_element_type=F32)
        ag = jax.nn.sigmoid(a0_ref[:, cs] + jnp.dot(alo, a2_ref[:, cs],
                                                    preferred_element_type=F32))
        kn = k * kk_ref[:, cs]
        kn = kn / jnp.maximum(jnp.sqrt(pair_sum(kn * kn)), 1e-12)
        r_p.append(r)
        k_p.append(k * (1.0 + (ag - 1.0) * ka_ref[:, cs]))
        v_p.append(v)
        kn_p.append(kn)
        bb_p.append(kn * ag)
        wl_p.append(-jnp.exp(-_softplus(-wraw) - 0.5))
        g_p.append(jnp.dot(glo, g2_ref[:, cs], preferred_element_type=F32))

    tail = slice(lb - SUBLANES, lb)
    prev_ref[:, 0:pw] = pr_ref[tail, :]
    prev_ref[:, pw:2 * pw] = pk_ref[tail, :]
    prev_ref[:, 2 * pw:3 * pw] = pv_ref[tail, :]
    prev_ref[:, 3 * pw:3 * pw + RWKV_LORA_W] = pl_ref[tail, :]

    def stack(x):
        return jnp.concatenate([jnp.where(head0_c, x, 0.0), jnp.where(head0_c, 0.0, x)], axis=0)

    tril = tril_ref[...]
    ms = ms_ref[...]
    mi = mi_ref[...]
    eye = eye_ref[...]
    two = 2 * lc
    probs = [(pi, ci) for ci in range(nchunk) for pi in range(npp)]
    ps = range(len(probs))

    def rows(vals):
        return [vals[pi][ci * lc:(ci + 1) * lc, :] for pi, ci in probs]

    wl = rows(wl_p)
    cum = [_mm_split_rhs(tril, x) for x in wl]
    cl = [x[lc - 1:lc, :] for x in cum]
    r, k, v, kn, bb = rows(r_p), rows(k_p), rows(v_p), rows(kn_p), rows(bb_p)
    e_n = [jnp.exp(-cum[i]) for i in ps]
    e_l = [jnp.exp(cl[i] - cum[i]) for i in ps]
    a_s = [stack(-kn[i] * jnp.exp(cum[i] - wl[i])) for i in ps]
    r_s = [stack(r[i] * jnp.exp(cum[i])) for i in ps]
    b_s = [stack(bb[i] * e_n[i]) for i in ps]
    k_s = [stack(k[i] * e_n[i]) for i in ps]
    v_s = [stack(v[i]) for i in ps]
    bp_t = [stack(bb[i] * e_l[i]).T for i in ps]
    kp_t = [stack(k[i] * e_l[i]).T for i in ps]

    gm = [_mm_nt(jnp.concatenate([a_s[i], r_s[i]], axis=0),
                 jnp.concatenate([b_s[i], k_s[i]], axis=0)) for i in ps]
    npow = [gm[i][:two, :two] * ms for i in ps]
    aak_v = [_mm(gm[i][:two, two:] * ms, v_s[i]) for i in ps]
    arb = [gm[i][two:, :two] * mi for i in ps]
    ark_v = [_mm(gm[i][two:, two:] * mi, v_s[i]) for i in ps]
    kp_v = [_mm(kp_t[i], v_s[i]) for i in ps]

    tinv = [eye + npow[i] for i in ps]
    for _ in range(int(np.log2(lc)) - 1):
        npow = [_mm(npow[i], npow[i]) for i in ps]
        tinv = [_mm(tinv[i], eye + npow[i]) for i in ps]

    taw = [_mm(tinv[i], jnp.concatenate([a_s[i], aak_v[i]], axis=1)) for i in ps]
    arb_taw = [_mm(arb[i], taw[i]) for i in ps]
    bp_taw = [_mm(bp_t[i], taw[i]) for i in ps]
    qh = [r_s[i] + arb_taw[i][:, :LANES] for i in ps]
    yin = [arb_taw[i][:, LANES:] + ark_v[i] for i in ps]
    phi = [eye * jnp.exp(cl[i]) + bp_taw[i][:, :LANES] for i in ps]
    psi = [bp_taw[i][:, LANES:] + kp_v[i] for i in ps]

    hst = [h_ref[pi] for pi in range(npp)]
    for i, (pi, ci) in enumerate(probs):
        ys = _mm(qh[i], hst[pi]) + yin[i]
        hst[pi] = _mm(phi[i], hst[pi]) + psi[i]
        y_ref[pi, ci * lc:(ci + 1) * lc, :] = ys[:lc] + ys[lc:]
    for pi in range(npp):
        h_ref[pi] = hst[pi]

    inv_head = 1.0 / RWKV_HEAD
    for pi in range(npp):
        cs = slice(pi * LANES, (pi + 1) * LANES)
        y = y_ref[pi]
        yc = y - pair_sum(y) * inv_head
        var = pair_sum(yc * yc) * inv_head
        yn = yc * lax.rsqrt(var + RWKV_LN_EPS) * lnw_ref[:, cs] + lnb_ref[:, cs]
        bonus = pair_sum(r_p[pi] * k_p[pi] * rk_ref[:, cs])
        o_ref[:, cs] = ((yn + bonus * v_p[pi]) * g_p[pi]).astype(o_ref.dtype)


def _rwkv_mix(p, mu, w0, a0, k_k, k_a, ln_w, ln_b, r_k, w2, a2, g2, n_b, t_len, d,
              nchunk=4, npp=4):
    lc = RWKV_CHUNK
    lb = lc * nchunk
    nb = t_len // lb
    pw = npp * LANES
    ngrp = d // pw
    two = 2 * lc
    lora_blk = (3 * d) // RWKV_LORA_W
    idx = np.arange(two)
    same = (idx[:, None] // lc) == (idx[None, :] // lc)
    ms = jnp.asarray((same & (idx[:, None] > idx[None, :])).astype(np.float32))
    mi = jnp.asarray((same & (idx[:, None] >= idx[None, :])).astype(np.float32))
    eye = jnp.asarray(np.eye(two, dtype=np.float32))
    tril = jnp.asarray(np.tril(np.ones((lc, lc), np.float32)), BF16)
    row = lambda b, t: b * nb + t
    pspec = lambda off: pl.BlockSpec((lb, pw), lambda b, g, t: (row(b, t), off + g))
    mspec = lambda off: pl.BlockSpec((1, pw), lambda b, g, t: (0, off + g))
    vspec = pl.BlockSpec((1, pw), lambda b, g, t: (0, g))
    wspec = lambda rows: pl.BlockSpec((rows, pw), lambda b, g, t: (0, g))
    sq = lambda s: pl.BlockSpec((s, s), lambda b, g, t: (0, 0))
    vec = lambda a: a.reshape(1, -1)
    mu2 = vec(mu)
    kern = functools.partial(_rwkv_kernel, lc=lc, nchunk=nchunk, npp=npp)
    return pl.pallas_call(
        kern,
        grid=(n_b, ngrp, nb),
        in_specs=[pspec(0), pspec(ngrp), pspec(2 * ngrp),
                  pl.BlockSpec((lb, RWKV_LORA_W), lambda b, g, t: (row(b, t), lora_blk)),
                  mspec(0), mspec(ngrp), mspec(2 * ngrp),
                  pl.BlockSpec((1, RWKV_LORA_W), lambda b, g, t: (0, lora_blk)),
                  vspec, vspec, vspec, vspec, vspec, vspec, vspec,
                  wspec(LANES), wspec(LANES), wspec(RWKV_GATE_LORA),
                  sq(lc), sq(two), sq(two), sq(two)],
        out_specs=pl.BlockSpec((lb, pw), lambda b, g, t: (row(b, t), g)),
        out_shape=jax.ShapeDtypeStruct((n_b * t_len, d), BF16),
        scratch_shapes=[pltpu.VMEM((npp, LANES, LANES), F32),
                        pltpu.VMEM((SUBLANES, 3 * pw + RWKV_LORA_W), F32),
                        pltpu.VMEM((lb + SUBLANES, LANES), F32),
                        pltpu.VMEM((npp, lb, LANES), F32)],
        compiler_params=_cparams("parallel", "parallel", "arbitrary"),
        name="rwkv_mix",
    )(p, p, p, p, mu2, mu2, mu2, mu2, vec(w0), vec(a0), vec(k_k), vec(k_a), vec(ln_w),
      vec(ln_b), vec(r_k), w2, a2, g2, tril, ms, mi, eye)


def _final_kernel(x_ref, g_ref, sh_ref, sc_ref, o_ref):
    o_ref[...] = _rms_mod(x_ref[...], g_ref[...], sh_ref[...], sc_ref[...])


def _final_norm(x2, g, mod5, t_len, tm=256):
    n, d = x2.shape
    tm = min(tm, t_len)
    tpb = t_len // tm
    return pl.pallas_call(
        _final_kernel,
        grid=(n // tm,),
        in_specs=[pl.BlockSpec((tm, d), lambda i: (i, 0)),
                  pl.BlockSpec((1, d), lambda i: (0, 0)),
                  _mod_spec(d, 0, 0, tpb),
                  _mod_spec(d, 0, 1, tpb)],
        out_specs=pl.BlockSpec((tm, d), lambda i: (i, 0)),
        out_shape=jax.ShapeDtypeStruct((n, d), F32),
        compiler_params=_cparams("parallel"),
        name="final_norm",
    )(x2, g.reshape(1, d), mod5, mod5)


def _pad_cols(w, width):
    return jnp.pad(w, ((0, 0), (0, width - w.shape[1])))


def _rwkv_pack_lora(lora_cols, mu, d):
    c1 = RWKV_DECAY_LORA
    c2 = c1 + RWKV_A_LORA
    def pack(a):
        return jnp.concatenate([_pad_cols(a[:, :c1], LANES), _pad_cols(a[:, c1:c2], LANES),
                                a[:, c2:]], axis=-1)
    mu2 = mu.reshape(1, -1)
    return pack(lora_cols), jnp.concatenate([mu2[:, :3 * d], pack(mu2[:, 3 * d:])], axis=-1)[0]


def _pad_rows(w, rows):
    return jnp.pad(w, ((0, rows - w.shape[0]), (0, 0)))


def kernel(x, c, mod_w, mod_b, norm_mix_g, norm_ffn_g, ret_w_in, ret_w_out, rwkv_w_in, rwkv_mu, rwkv_w0, rwkv_w2, rwkv_a0, rwkv_a2, rwkv_g2, rwkv_k_k, rwkv_k_a, rwkv_r_k, rwkv_ln_w, rwkv_ln_b, rwkv_w_out, mlstm_w_in, mlstm_conv_w, mlstm_conv_b, mlstm_gate_b, mlstm_norm_g, mlstm_w_out, ffn_w_up, ffn_conv_w, ffn_conv_b, ffn_w_down, final_g, final_mod_w, final_mod_b):
    n_b, t_len, d = x.shape
    depth = mod_w.shape[0]
    x2 = x.reshape(n_b * t_len, d)

    c_pad = jnp.pad(c, ((0, SUBLANES - n_b), (0, 0)))
    mod5 = _mod_vectors(c_pad, mod_w, mod_b).reshape(depth, SUBLANES, 6, 1, d)
    fmod5 = _mod_vectors(c_pad, final_mod_w[None], final_mod_b[None]).reshape(1, SUBLANES, 2, 1, d)
    mix, ffn = 0, 3

    ret_w_out16 = ret_w_out.astype(BF16)
    rwkv_w_out16 = rwkv_w_out.astype(BF16)
    mlstm_w_out16 = mlstm_w_out.astype(BF16)
    ffn_w_down16 = ffn_w_down.astype(BF16)

    for i in range(depth):
        kind, j = i % N_MIXERS, i // N_MIXERS
        if kind == 0:
            qkvg = _nm_linear(x2, norm_mix_g[i], mod5, i, mix, ret_w_in, j, t_len, BF16)
            o = _retention(qkvg, n_b, t_len, d)
            x2 = _linear_res(o, ret_w_out16, j, x2, mod5, i, mix + 2, t_len)
        elif kind == 1:
            lora_w, mu = _rwkv_pack_lora(rwkv_w_in[j, :, 3 * d:], rwkv_mu[j], d)
            p = _nm_linear(x2, norm_mix_g[i], mod5, i, mix, rwkv_w_in, j, t_len, F32,
                           w_tail=lora_w)
            o = _rwkv_mix(p, mu, rwkv_w0[j], rwkv_a0[j], rwkv_k_k[j], rwkv_k_a[j], rwkv_ln_w[j],
                          rwkv_ln_b[j], rwkv_r_k[j],
                          _pad_rows(rwkv_w2[j], LANES).astype(BF16),
                          _pad_rows(rwkv_a2[j], LANES).astype(BF16),
                          rwkv_g2[j].astype(BF16), n_b, t_len, d)
            x2 = _linear_res(o, rwkv_w_out16, j, x2, mod5, i, mix + 2, t_len)
        else:
            gate_w = _pad_cols(mlstm_w_in[j, :, 3 * d:], 4 * LANES)
            proj = _nm_linear(x2, norm_mix_g[i], mod5, i, mix, mlstm_w_in, j, t_len, F32,
                              w_tail=gate_w)
            o = _mlstm(proj, mlstm_conv_w[j], mlstm_conv_b[j], mlstm_gate_b[j], mlstm_norm_g[j],
                       n_b, t_len, d)
            x2 = _linear_res(o, mlstm_w_out16, j, x2, mod5, i, mix + 2, t_len)
        act = _ffn_up(x2, norm_ffn_g[i], mod5, ffn, ffn_w_up, ffn_conv_w, ffn_conv_b, i, t_len)
        x2 = _linear_res(act, ffn_w_down16, i, x2, mod5, i, ffn + 2, t_len)

    out = _final_norm(x2, final_g, fmod5, t_len)
    return out.reshape(n_b, t_len, d)
```

```python
import functools

import numpy as np
import jax
import jax.numpy as jnp
from jax import lax
from jax.experimental import pallas as pl
from jax.experimental.pallas import tpu as pltpu

F32 = jnp.float32
BF16 = jnp.bfloat16

V7X_VMEM_BYTES = 64 * 1024 * 1024
VMEM_LIMIT = (V7X_VMEM_BYTES * 7) // 8
LANES = 128
SUBLANES = 8
BF16_ROWS = 16
NORM_ROWS = 16

EPS = 1e-6
N_MIXERS = 3

RET_HEADS = 8
ROPE_BASE = 10000.0
RET_CHUNK = 128

RWKV_HEAD = 64
RWKV_DECAY_LORA = 96
RWKV_A_LORA = 96
RWKV_GATE_LORA = 256
RWKV_LORA_W = 2 * LANES + RWKV_GATE_LORA
RWKV_LN_EPS = 64e-5
RWKV_CHUNK = 64

MLSTM_HEADS = 4
MLSTM_CONV = 4
MLSTM_CHUNK = 128
GATE_CAP = 15.0

FFN_CONV = 3
NEG_BIG = -1e30


def _cparams(*sem):
    return pltpu.CompilerParams(dimension_semantics=sem, vmem_limit_bytes=VMEM_LIMIT)


def _mm(a, b):
    return jnp.dot(a.astype(BF16), b.astype(BF16), preferred_element_type=F32)


def _mm_nt(a, b):
    return lax.dot_general(a.astype(BF16), b.astype(BF16), (((1,), (1,)), ((), ())),
                           preferred_element_type=F32)


def _split(x):
    hi = x.astype(BF16)
    lo = (x - hi.astype(F32)).astype(BF16)
    return hi, lo


def _mm_split_rhs(m, x):
    hi, lo = _split(x)
    return (jnp.dot(m, hi, preferred_element_type=F32)
            + jnp.dot(m, lo, preferred_element_type=F32))


def _silu(x):
    return x * jax.nn.sigmoid(x)


def _softplus(z):
    return jnp.maximum(z, 0.0) + jnp.log(1.0 + jnp.exp(-jnp.abs(z)))


def _rms_mod(x, g, shift, scale):
    y = x * lax.rsqrt(jnp.mean(x * x, axis=-1, keepdims=True) + EPS)
    return (y * g) * (1.0 + scale) + shift


def _mod_kernel(c_ref, w_ref, b_ref, o_ref):
    hi, lo = _split(_silu(c_ref[...]))
    w = w_ref[0].astype(BF16)
    o_ref[0] = (jnp.dot(hi, w, preferred_element_type=F32)
                + jnp.dot(lo, w, preferred_element_type=F32)) + b_ref[0]


def _mod_vectors(c_pad, w, b, tn=1024):
    depth, d, n = w.shape
    rows = c_pad.shape[0]
    return pl.pallas_call(
        _mod_kernel,
        grid=(depth, n // tn),
        in_specs=[pl.BlockSpec((rows, d), lambda i, j: (0, 0)),
                  pl.BlockSpec((1, d, tn), lambda i, j: (i, 0, j)),
                  pl.BlockSpec((1, 1, tn), lambda i, j: (i, 0, j))],
        out_specs=pl.BlockSpec((1, rows, tn), lambda i, j: (i, 0, j)),
        out_shape=jax.ShapeDtypeStruct((depth, rows, n), F32),
        compiler_params=_cparams("parallel", "parallel"),
        name="mod_vectors",
    )(c_pad, w, b.reshape(depth, 1, n))


def _mod_spec(d, layer, which, tpb, tn=None):
    if tn is None:
        return pl.BlockSpec((None, None, None, 1, d),
                            lambda i, *_: (layer, i // tpb, which, 0, 0))
    return pl.BlockSpec((None, None, None, 1, tn), lambda i, j: (layer, i // tpb, which, 0, j))


def _nm_linear_kernel(x_ref, g_ref, sh_ref, sc_ref, w_ref, *rest, tm, rchunk, has_tail):
    wt_ref = rest[0] if has_tail else None
    o_ref, h_ref = rest[-2:]
    col_tile = pl.program_id(1)
    last = pl.num_programs(1) - 1

    @pl.when(col_tile == 0)
    def _():
        for r0 in range(0, tm, rchunk):
            h = _rms_mod(x_ref[r0:r0 + rchunk, :], g_ref[...], sh_ref[...], sc_ref[...])
            h_ref[r0:r0 + rchunk, :] = h.astype(BF16)

    def project(wref):
        o_ref[...] = jnp.dot(h_ref[...], wref[...].astype(BF16),
                             preferred_element_type=F32).astype(o_ref.dtype)

    if has_tail:
        pl.when(col_tile < last)(lambda: project(w_ref))
        pl.when(col_tile == last)(lambda: project(wt_ref))
    else:
        project(w_ref)


def _nm_linear(x2, g, mod5, mlayer, sh_idx, w, wlayer, t_len, out_dtype, w_tail=None, tm=1024):
    n, d = x2.shape
    tn = 1024 if (w_tail is None and w.shape[2] % 1024 == 0) else 512
    nmain = w.shape[2] // tn
    ncol = nmain + (0 if w_tail is None else w_tail.shape[1] // tn)
    tm = min(tm, t_len)
    tpb = t_len // tm
    kern = functools.partial(_nm_linear_kernel, tm=tm, rchunk=min(NORM_ROWS, tm),
                             has_tail=w_tail is not None)
    in_specs = [pl.BlockSpec((tm, d), lambda i, j: (i, 0)),
                pl.BlockSpec((1, d), lambda i, j: (0, 0)),
                _mod_spec(d, mlayer, sh_idx, tpb),
                _mod_spec(d, mlayer, sh_idx + 1, tpb),
                pl.BlockSpec((None, d, tn), lambda i, j: (wlayer, 0, jnp.minimum(j, nmain - 1)))]
    args = [x2, g.reshape(1, d), mod5, mod5, w]
    if w_tail is not None:
        in_specs.append(pl.BlockSpec((d, tn), lambda i, j: (0, jnp.maximum(j - nmain, 0))))
        args.append(w_tail)
    return pl.pallas_call(
        kern,
        grid=(n // tm, ncol),
        in_specs=in_specs,
        out_specs=pl.BlockSpec((tm, tn), lambda i, j: (i, j)),
        out_shape=jax.ShapeDtypeStruct((n, ncol * tn), out_dtype),
        scratch_shapes=[pltpu.VMEM((tm, d), BF16)],
        compiler_params=_cparams("parallel", "arbitrary"),
        name="norm_mod_linear",
    )(*args)


def _ffn_up_kernel(x_ref, xp_ref, g_ref, sh_ref, sc_ref, wg_ref, wv_ref, cw_ref, cb_ref,
                   o_ref, h_ref, ext_ref, *, tm, rchunk, tpb):
    first = (pl.program_id(0) % tpb) == 0
    col_tile = pl.program_id(1)
    halo = BF16_ROWS

    @pl.when(col_tile == 0)
    def _():
        hp = _rms_mod(xp_ref[...], g_ref[...], sh_ref[...], sc_ref[...])
        h_ref[0:halo, :] = hp.astype(BF16)
        for r0 in range(0, tm, NORM_ROWS):
            h = _rms_mod(x_ref[r0:r0 + NORM_ROWS, :], g_ref[...], sh_ref[...], sc_ref[...])
            h_ref[halo + r0:halo + r0 + NORM_ROWS, :] = h.astype(BF16)

    ext_ref[...] = jnp.dot(h_ref[...], wg_ref[...].astype(BF16), preferred_element_type=F32)
    ext_ref[0:halo, :] = jnp.where(first, 0.0, ext_ref[0:halo, :])
    val = jnp.dot(h_ref[halo:halo + tm, :], wv_ref[...].astype(BF16),
                  preferred_element_type=F32)
    for r0 in range(0, tm, rchunk):
        base = halo + r0
        conv = (cw_ref[0:1, :] * ext_ref[base - 2:base - 2 + rchunk, :]
                + cw_ref[1:2, :] * ext_ref[base - 1:base - 1 + rchunk, :]
                + cw_ref[2:3, :] * ext_ref[base:base + rchunk, :]
                + cb_ref[...])
        o_ref[r0:r0 + rchunk, :] = (_silu(conv) * val[r0:r0 + rchunk, :]).astype(o_ref.dtype)


def _ffn_up(x2, g, mod5, sh_idx, w_up, conv_w, conv_b, layer, t_len, tm=1024, tn=512):
    n, d = x2.shape
    f = w_up.shape[2] // 2
    tm = min(tm, t_len)
    tpb = t_len // tm
    rb = tm // BF16_ROWS
    nf = f // tn
    kern = functools.partial(_ffn_up_kernel, tm=tm, rchunk=min(256, tm), tpb=tpb)
    return pl.pallas_call(
        kern,
        grid=(n // tm, nf),
        in_specs=[pl.BlockSpec((tm, d), lambda i, j: (i, 0)),
                  pl.BlockSpec((BF16_ROWS, d), lambda i, j: (jnp.maximum(i * rb - 1, 0), 0)),
                  pl.BlockSpec((1, d), lambda i, j: (0, 0)),
                  _mod_spec(d, layer, sh_idx, tpb),
                  _mod_spec(d, layer, sh_idx + 1, tpb),
                  pl.BlockSpec((None, d, tn), lambda i, j: (layer, 0, j)),
                  pl.BlockSpec((None, d, tn), lambda i, j: (layer, 0, nf + j)),
                  pl.BlockSpec((None, FFN_CONV, tn), lambda i, j: (layer, 0, j)),
                  pl.BlockSpec((None, 1, tn), lambda i, j: (layer, 0, j))],
        out_specs=pl.BlockSpec((tm, tn), lambda i, j: (i, j)),
        out_shape=jax.ShapeDtypeStruct((n, f), BF16),
        scratch_shapes=[pltpu.VMEM((tm + BF16_ROWS, d), BF16),
                        pltpu.VMEM((tm + BF16_ROWS, tn), F32)],
        compiler_params=_cparams("parallel", "arbitrary"),
        name="ffn_up_act",
    )(x2, x2, g.reshape(1, d), mod5, mod5, w_up, w_up, conv_w,
      conv_b.reshape(conv_b.shape[0], 1, f))


def _linear_res_kernel(a_ref, w_ref, x_ref, gate_ref, o_ref):
    acc = jnp.dot(a_ref[...], w_ref[...], preferred_element_type=F32)
    o_ref[...] = x_ref[...] + gate_ref[...] * acc


def _linear_res(a, w, wlayer, x2, mod5, mlayer, gate_idx, t_len, tm=1024):
    n, k = a.shape
    m = w.shape[2]
    tn = 1024 if k <= 2048 else 512
    tm = min(tm, t_len)
    tpb = t_len // tm
    return pl.pallas_call(
        _linear_res_kernel,
        grid=(n // tm, m // tn),
        in_specs=[pl.BlockSpec((tm, k), lambda i, j: (i, 0)),
                  pl.BlockSpec((None, k, tn), lambda i, j: (wlayer, 0, j)),
                  pl.BlockSpec((tm, tn), lambda i, j: (i, j)),
                  _mod_spec(m, mlayer, gate_idx, tpb, tn)],
        out_specs=pl.BlockSpec((tm, tn), lambda i, j: (i, j)),
        out_shape=jax.ShapeDtypeStruct((n, m), F32),
        compiler_params=_cparams("parallel", "arbitrary"),
        name="linear_residual",
    )(a, w, x2, mod5)


def _retention_kernel(q_ref, k_ref, v_ref, g_ref, cos_ref, sin_ref, dint_ref, dq_ref, dk_ref,
                      dc_ref, o_ref, s_ref, *, heads, dk, dv):
    chunk = pl.program_id(1)

    @pl.when(chunk == 0)
    def _():
        s_ref[...] = jnp.zeros_like(s_ref)

    cos = cos_ref[...]
    sin = sin_ref[...]
    half = dk // 2

    def rot(ref, h):
        x1 = ref[:, h * dk:h * dk + half].astype(F32)
        x2 = ref[:, h * dk + half:(h + 1) * dk].astype(F32)
        return jnp.concatenate([x1 * cos - x2 * sin, x1 * sin + x2 * cos], axis=-1)

    hs = range(heads)
    q = [rot(q_ref, h) for h in hs]
    k = [rot(k_ref, h) * (dk ** -0.5) for h in hs]
    v = [v_ref[:, h * dv:(h + 1) * dv] for h in hs]
    scores = [_mm_nt(q[h], k[h]) * dint_ref[h] for h in hs]
    state = [s_ref[h] for h in hs]
    out = [_mm(scores[h], v[h]) + _mm(q[h] * dq_ref[h], state[h]) for h in hs]
    for h in hs:
        s_ref[h] = dc_ref[h] * state[h] + _mm((k[h] * dk_ref[h]).T, v[h])
    for h in hs:
        oc = out[h] - jnp.mean(out[h], axis=-1, keepdims=True)
        o = oc * lax.rsqrt(jnp.mean(oc * oc, axis=-1, keepdims=True) + EPS)
        gate = g_ref[:, h * dv:(h + 1) * dv].astype(F32)
        o_ref[:, h * dv:(h + 1) * dv] = (o * _silu(gate)).astype(o_ref.dtype)


def _retention_tables(t_len, dk):
    half = dk // 2
    inv = ROPE_BASE ** (-np.arange(half, dtype=np.float64) / half)
    ang = np.arange(t_len, dtype=np.float64)[:, None] * inv[None, :]
    lg = np.log1p(-np.power(2.0, -5.0 - np.arange(RET_HEADS, dtype=np.float64)))
    idx = np.arange(RET_CHUNK, dtype=np.float64)
    diff = idx[:, None] - idx[None, :]
    dint = np.where(diff >= 0, np.exp(lg[:, None, None] * np.maximum(diff, 0.0)), 0.0)
    dq = np.exp(lg[:, None] * (idx + 1.0))[..., None]
    dkk = np.exp(lg[:, None] * (RET_CHUNK - 1.0 - idx))[..., None]
    dc = np.exp(lg * RET_CHUNK)[:, None, None]
    f = lambda a: jnp.asarray(a, F32)
    return f(np.cos(ang)), f(np.sin(ang)), f(dint), f(dq), f(dkk), f(dc)


def _retention(qkvg, n_b, t_len, d):
    n = qkvg.shape[0]
    heads = RET_HEADS
    dk = d // heads
    dv = 2 * dk
    lc = RET_CHUNK
    nc = t_len // lc
    cos, sin, dint, dq, dkk, dc = _retention_tables(t_len, dk)
    row = lambda b, c: b * nc + c
    full = lambda a: pl.BlockSpec(a.shape, lambda b, c: (0,) * a.ndim)
    kern = functools.partial(_retention_kernel, heads=heads, dk=dk, dv=dv)
    return pl.pallas_call(
        kern,
        grid=(n_b, nc),
        in_specs=[pl.BlockSpec((lc, d), lambda b, c: (row(b, c), 0)),
                  pl.BlockSpec((lc, d), lambda b, c: (row(b, c), 1)),
                  pl.BlockSpec((lc, 2 * d), lambda b, c: (row(b, c), 1)),
                  pl.BlockSpec((lc, 2 * d), lambda b, c: (row(b, c), 2)),
                  pl.BlockSpec((lc, dk // 2), lambda b, c: (c, 0)),
                  pl.BlockSpec((lc, dk // 2), lambda b, c: (c, 0)),
                  full(dint), full(dq), full(dkk), full(dc)],
        out_specs=pl.BlockSpec((lc, 2 * d), lambda b, c: (row(b, c), 0)),
        out_shape=jax.ShapeDtypeStruct((n, 2 * d), BF16),
        scratch_shapes=[pltpu.VMEM((heads, dk, dv), F32)],
        compiler_params=_cparams("parallel", "arbitrary"),
        name="retention_chunk",
    )(qkvg, qkvg, qkvg, qkvg, cos, sin, dint, dq, dkk, dc)


def _mlstm_kernel(q_ref, k_ref, v_ref, og_ref, gt_ref, cw_ref, cb_ref, gb_ref, ng_ref, tril_ref,
                  o_ref, qext, kext, c_ref, n_ref, m_ref, *, lc, dqk, dv, heads):
    chunk = pl.program_id(1)
    qw = heads * dqk

    @pl.when(chunk == 0)
    def _():
        qext[0:SUBLANES, :] = jnp.zeros((SUBLANES, qw), F32)
        kext[0:SUBLANES, :] = jnp.zeros((SUBLANES, qw), F32)
        c_ref[...] = jnp.zeros_like(c_ref)
        n_ref[...] = jnp.zeros_like(n_ref)
        m_ref[...] = jnp.zeros_like(m_ref)

    def conv_silu(raw_ref, ext, c0):
        ext[SUBLANES:SUBLANES + lc, :] = raw_ref[...]
        y = cb_ref[:, c0:c0 + qw]
        for j in range(MLSTM_CONV):
            off = SUBLANES - (MLSTM_CONV - 1) + j
            y = y + cw_ref[j:j + 1, c0:c0 + qw] * ext[off:off + lc, :]
        ext[0:SUBLANES, :] = ext[lc:lc + SUBLANES, :]
        return _silu(y)

    q_all = conv_silu(q_ref, qext, 0)
    k_all = conv_silu(k_ref, kext, qw) * (dqk ** -0.5)

    gt = gt_ref[...] + gb_ref[...]
    lane = lax.broadcasted_iota(jnp.int32, gt.shape, 1)
    tril = tril_ref[...]
    causal = (lax.broadcasted_iota(jnp.int32, (lc, lc), 0)
              >= lax.broadcasted_iota(jnp.int32, (lc, lc), 1))

    hs = range(heads)
    q = [q_all[:, h * dqk:(h + 1) * dqk] for h in hs]
    k = [k_all[:, h * dqk:(h + 1) * dqk] for h in hs]
    v = [v_ref[:, h * dv:(h + 1) * dv] for h in hs]
    ipre = [jnp.sum(jnp.where(lane == h, gt, 0.0), axis=-1, keepdims=True) for h in hs]
    fpre = [jnp.sum(jnp.where(lane == h + heads, gt, 0.0), axis=-1, keepdims=True) for h in hs]
    li = [GATE_CAP * jnp.tanh(x / GATE_CAP) for x in ipre]
    lf = [-_softplus(-(GATE_CAP * jnp.tanh(x / GATE_CAP))) for x in fpre]
    b_full = [_mm_split_rhs(tril, jnp.broadcast_to(x, (lc, lc))) for x in lf]
    b_col = [x[:, 0:1] for x in b_full]
    dmat = [jnp.where(causal, b_full[h] - b_full[h].T + jnp.broadcast_to(li[h], (lc, lc)).T,
                      NEG_BIG) for h in hs]
    m_st = [m_ref[h] for h in hs]
    m_inter = [b_col[h] + m_st[h] for h in hs]
    m_t = [jnp.maximum(m_inter[h], jnp.max(dmat[h], axis=-1, keepdims=True)) for h in hs]
    scores = [_mm_nt(q[h], k[h]) * jnp.exp(dmat[h] - m_t[h]) for h in hs]
    w_inter = [jnp.exp(m_inter[h] - m_t[h]) for h in hs]
    c_st = [c_ref[h] for h in hs]
    n_st = [n_ref[h] for h in hs]
    num = [_mm(scores[h], v[h]) + w_inter[h] * _mm(q[h], c_st[h]) for h in hs]
    den = [jnp.sum(scores[h], axis=-1, keepdims=True)
           + w_inter[h] * jnp.sum(q[h] * n_st[h], axis=-1, keepdims=True) for h in hs]
    for h in hs:
        b_last = b_col[h][lc - 1:lc, :]
        gdec = b_last - b_col[h] + li[h]
        m_new = jnp.maximum(b_last + m_st[h], jnp.max(gdec, axis=0, keepdims=True))
        wk = jnp.exp(gdec - m_new)
        carry = jnp.exp(b_last + m_st[h] - m_new)
        kw = k[h] * wk
        c_ref[h] = carry * c_st[h] + _mm(kw.T, v[h])
        n_ref[h] = carry * n_st[h] + jnp.sum(kw, axis=0, keepdims=True)
        m_ref[h] = m_new
    for h in hs:
        hh = num[h] / jnp.maximum(jnp.abs(den[h]), jnp.exp(-m_t[h]))
        cs = slice(h * dv, (h + 1) * dv)
        hn = hh * lax.rsqrt(jnp.mean(hh * hh, axis=-1, keepdims=True) + EPS) * ng_ref[:, cs]
        o_ref[:, cs] = (hn * jax.nn.sigmoid(og_ref[:, cs])).astype(o_ref.dtype)


def _mlstm(proj, conv_w, conv_b, gate_b, norm_g, n_b, t_len, d):
    n = proj.shape[0]
    heads = MLSTM_HEADS
    dqk = d // (2 * heads)
    dv = d // heads
    lc = MLSTM_CHUNK
    nc = t_len // lc
    qw = heads * dqk
    gtb = (2 * qw + 2 * d) // LANES
    gb = jnp.zeros((1, LANES), F32).at[0, :2 * heads].set(gate_b.astype(F32))
    tril = jnp.asarray(np.tril(np.ones((lc, lc), np.float32)), BF16)
    kern = functools.partial(_mlstm_kernel, lc=lc, dqk=dqk, dv=dv, heads=heads)
    row = lambda b, c: b * nc + c
    full = lambda a: pl.BlockSpec(a.shape, lambda b, c: (0,) * a.ndim)
    cb = conv_b.reshape(1, 2 * qw)
    ng = norm_g.reshape(1, d)
    return pl.pallas_call(
        kern,
        grid=(n_b, nc),
        in_specs=[pl.BlockSpec((lc, qw), lambda b, c: (row(b, c), 0)),
                  pl.BlockSpec((lc, qw), lambda b, c: (row(b, c), 1)),
                  pl.BlockSpec((lc, d), lambda b, c: (row(b, c), (2 * qw) // d)),
                  pl.BlockSpec((lc, d), lambda b, c: (row(b, c), (2 * qw) // d + 1)),
                  pl.BlockSpec((lc, LANES), lambda b, c: (row(b, c), gtb)),
                  full(conv_w), full(cb), full(gb), full(ng), full(tril)],
        out_specs=pl.BlockSpec((lc, d), lambda b, c: (row(b, c), 0)),
        out_shape=jax.ShapeDtypeStruct((n, d), BF16),
        scratch_shapes=[pltpu.VMEM((lc + SUBLANES, qw), F32),
                        pltpu.VMEM((lc + SUBLANES, qw), F32),
                        pltpu.VMEM((heads, dqk, dv), F32),
                        pltpu.VMEM((heads, 1, dqk), F32),
                        pltpu.VMEM((heads, 1, 1), F32)],
        compiler_params=_cparams("parallel", "arbitrary"),
        name="mlstm_chunk",
    )(proj, proj, proj, proj, proj, conv_w, cb, gb, ng, tril)


def _rwkv_kernel(pr_ref, pk_ref, pv_ref, pl_ref, mur_ref, muk_ref, muv_ref, mul_ref,
                 w0_ref, a0_ref, kk_ref, ka_ref, lnw_ref, lnb_ref, rk_ref,
                 w2_ref, a2_ref, g2_ref, tril_ref, ms_ref, mi_ref, eye_ref,
                 o_ref, h_ref, prev_ref, ext_ref, y_ref, *, lc, nchunk, npp):
    block = pl.program_id(2)
    lb = lc * nchunk
    pw = npp * LANES

    @pl.when(block == 0)
    def _():
        h_ref[...] = jnp.zeros_like(h_ref)
        prev_ref[...] = jnp.zeros_like(prev_ref)

    lane = lax.broadcasted_iota(jnp.int32, (lb, LANES), 1)
    head0 = lane < RWKV_HEAD
    head0_c = lax.broadcasted_iota(jnp.int32, (lc, LANES), 1) < RWKV_HEAD

    def pair_sum(x):
        s0 = jnp.sum(jnp.where(head0, x, 0.0), axis=-1, keepdims=True)
        s1 = jnp.sum(jnp.where(head0, 0.0, x), axis=-1, keepdims=True)
        return jnp.where(head0, s0, s1)

    def shifted(src_ref, mu_ref, c0, pc0):
        cs = slice(c0, c0 + LANES)
        ext_ref[0:SUBLANES, :] = prev_ref[:, pc0:pc0 + LANES]
        ext_ref[SUBLANES:SUBLANES + lb, :] = src_ref[:, cs]
        cur = src_ref[:, cs]
        return cur + mu_ref[:, cs] * (ext_ref[SUBLANES - 1:SUBLANES - 1 + lb, :] - cur)

    lo0 = 3 * pw
    wlo = jnp.tanh(shifted(pl_ref, mul_ref, 0, lo0)).astype(BF16)
    alo = shifted(pl_ref, mul_ref, LANES, lo0 + LANES).astype(BF16)
    glo = jax.nn.sigmoid(jnp.concatenate(
        [shifted(pl_ref, mul_ref, 2 * LANES + s, lo0 + 2 * LANES + s)
         for s in range(0, RWKV_GATE_LORA, LANES)], axis=-1)).astype(BF16)

    r_p, k_p, v_p, kn_p, bb_p, wl_p, g_p = [], [], [], [], [], [], []
    for pi in range(npp):
        cs = slice(pi * LANES, (pi + 1) * LANES)
        r = shifted(pr_ref, mur_ref, pi * LANES, pi * LANES)
        k = shifted(pk_ref, muk_ref, pi * LANES, pw + pi * LANES)
        v = shifted(pv_ref, muv_ref, pi * LANES, 2 * pw + pi * LANES)
        wraw = w0_ref[:, cs] + jnp.dot(wlo, w2_ref[:, cs], preferred_element_type=F32)
        ag = jax.nn.sigmoid(a0_ref[:, cs] + jnp.dot(alo, a2_ref[:, cs],
                                                    preferred_element_type=F32))
        kn = k * kk_ref[:, cs]
        kn = kn / jnp.maximum(jnp.sqrt(pair_sum(kn * kn)), 1e-12)
        r_p.append(r)
        k_p.append(k * (1.0 + (ag - 1.0) * ka_ref[:, cs]))
        v_p.append(v)
        kn_p.append(kn)
        bb_p.append(kn * ag)
        wl_p.append(-jnp.exp(-_softplus(-wraw) - 0.5))
        g_p.append(jnp.dot(glo, g2_ref[:, cs], preferred_element_type=F32))

    tail = slice(lb - SUBLANES, lb)
    prev_ref[:, 0:pw] = pr_ref[tail, :]
    prev_ref[:, pw:2 * pw] = pk_ref[tail, :]
    prev_ref[:, 2 * pw:3 * pw] = pv_ref[tail, :]
    prev_ref[:, 3 * pw:3 * pw + RWKV_LORA_W] = pl_ref[tail, :]

    def stack(x):
        return jnp.concatenate([jnp.where(head0_c, x, 0.0), jnp.where(head0_c, 0.0, x)], axis=0)

    tril = tril_ref[...]
    ms = ms_ref[...]
    mi = mi_ref[...]
    eye = eye_ref[...]
    two = 2 * lc
    probs = [(pi, ci) for ci in range(nchunk) for pi in range(npp)]
    ps = range(len(probs))

    def rows(vals):
        return [vals[pi][ci * lc:(ci + 1) * lc, :] for pi, ci in probs]

    wl = rows(wl_p)
    cum = [_mm_split_rhs(tril, x) for x in wl]
    cl = [x[lc - 1:lc, :] for x in cum]
    r, k, v, kn, bb = rows(r_p), rows(k_p), rows(v_p), rows(kn_p), rows(bb_p)
    e_n = [jnp.exp(-cum[i]) for i in ps]
    e_l = [jnp.exp(cl[i] - cum[i]) for i in ps]
    a_s = [stack(-kn[i] * jnp.exp(cum[i] - wl[i])) for i in ps]
    r_s = [stack(r[i] * jnp.exp(cum[i])) for i in ps]
    b_s = [stack(bb[i] * e_n[i]) for i in ps]
    k_s = [stack(k[i] * e_n[i]) for i in ps]
    v_s = [stack(v[i]) for i in ps]
    bp_t = [stack(bb[i] * e_l[i]).T for i in ps]
    kp_t = [stack(k[i] * e_l[i]).T for i in ps]

    gm = [_mm_nt(jnp.concatenate([a_s[i], r_s[i]], axis=0),
                 jnp.concatenate([b_s[i], k_s[i]], axis=0)) for i in ps]
    npow = [gm[i][:two, :two] * ms for i in ps]
    aak_v = [_mm(gm[i][:two, two:] * ms, v_s[i]) for i in ps]
    arb = [gm[i][two:, :two] * mi for i in ps]
    ark_v = [_mm(gm[i][two:, two:] * mi, v_s[i]) for i in ps]
    kp_v = [_mm(kp_t[i], v_s[i]) for i in ps]

    tinv = [eye + npow[i] for i in ps]
    for _ in range(int(np.log2(lc)) - 1):
        npow = [_mm(npow[i], npow[i]) for i in ps]
        tinv = [_mm(tinv[i], eye + npow[i]) for i in ps]

    taw = [_mm(tinv[i], jnp.concatenate([a_s[i], aak_v[i]], axis=1)) for i in ps]
    arb_taw = [_mm(arb[i], taw[i]) for i in ps]
    bp_taw = [_mm(bp_t[i], taw[i]) for i in ps]
    qh = [r_s[i] + arb_taw[i][:, :LANES] for i in ps]
    yin = [arb_taw[i][:, LANES:] + ark_v[i] for i in ps]
    phi = [eye * jnp.exp(cl[i]) + bp_taw[i][:, :LANES] for i in ps]
    psi = [bp_taw[i][:, LANES:] + kp_v[i] for i in ps]

    hst = [h_ref[pi] for pi in range(npp)]
    for i, (pi, ci) in enumerate(probs):
        ys = _mm(qh[i], hst[pi]) + yin[i]
        hst[pi] = _mm(phi[i], hst[pi]) + psi[i]
        y_ref[pi, ci * lc:(ci + 1) * lc, :] = ys[:lc] + ys[lc:]
    for pi in range(npp):
        h_ref[pi] = hst[pi]

    inv_head = 1.0 / RWKV_HEAD
    for pi in range(npp):
        cs = slice(pi * LANES, (pi + 1) * LANES)
        y = y_ref[pi]
        yc = y - pair_sum(y) * inv_head
        var = pair_sum(yc * yc) * inv_head
        yn = yc * lax.rsqrt(var + RWKV_LN_EPS) * lnw_ref[:, cs] + lnb_ref[:, cs]
        bonus = pair_sum(r_p[pi] * k_p[pi] * rk_ref[:, cs])
        o_ref[:, cs] = ((yn + bonus * v_p[pi]) * g_p[pi]).astype(o_ref.dtype)


def _rwkv_mix(p, mu, w0, a0, k_k, k_a, ln_w, ln_b, r_k, w2, a2, g2, n_b, t_len, d,
              nchunk=4, npp=4):
    lc = RWKV_CHUNK
    lb = lc * nchunk
    nb = t_len // lb
    pw = npp * LANES
    ngrp = d // pw
    two = 2 * lc
    lora_blk = (3 * d) // RWKV_LORA_W
    idx = np.arange(two)
    same = (idx[:, None] // lc) == (idx[None, :] // lc)
    ms = jnp.asarray((same & (idx[:, None] > idx[None, :])).astype(np.float32))
    mi = jnp.asarray((same & (idx[:, None] >= idx[None, :])).astype(np.float32))
    eye = jnp.asarray(np.eye(two, dtype=np.float32))
    tril = jnp.asarray(np.tril(np.ones((lc, lc), np.float32)), BF16)
    row = lambda b, t: b * nb + t
    pspec = lambda off: pl.BlockSpec((lb, pw), lambda b, g, t: (row(b, t), off + g))
    mspec = lambda off: pl.BlockSpec((1, pw), lambda b, g, t: (0, off + g))
    vspec = pl.BlockSpec((1, pw), lambda b, g, t: (0, g))
    wspec = lambda rows: pl.BlockSpec((rows, pw), lambda b, g, t: (0, g))
    sq = lambda s: pl.BlockSpec((s, s), lambda b, g, t: (0, 0))
    vec = lambda a: a.reshape(1, -1)
    mu2 = vec(mu)
    kern = functools.partial(_rwkv_kernel, lc=lc, nchunk=nchunk, npp=npp)
    return pl.pallas_call(
        kern,
        grid=(n_b, ngrp, nb),
        in_specs=[pspec(0), pspec(ngrp), pspec(2 * ngrp),
                  pl.BlockSpec((lb, RWKV_LORA_W), lambda b, g, t: (row(b, t), lora_blk)),
                  mspec(0), mspec(ngrp), mspec(2 * ngrp),
                  pl.BlockSpec((1, RWKV_LORA_W), lambda b, g, t: (0, lora_blk)),
                  vspec, vspec, vspec, vspec, vspec, vspec, vspec,
                  wspec(LANES), wspec(LANES), wspec(RWKV_GATE_LORA),
                  sq(lc), sq(two), sq(two), sq(two)],
        out_specs=pl.BlockSpec((lb, pw), lambda b, g, t: (row(b, t), g)),
        out_shape=jax.ShapeDtypeStruct((n_b * t_len, d), BF16),
        scratch_shapes=[pltpu.VMEM((npp, LANES, LANES), F32),
                        pltpu.VMEM((SUBLANES, 3 * pw + RWKV_LORA_W), F32),
                        pltpu.VMEM((lb + SUBLANES, LANES), F32),
                        pltpu.VMEM((npp, lb, LANES), F32)],
        compiler_params=_cparams("parallel", "parallel", "arbitrary"),
        name="rwkv_mix",
    )(p, p, p, p, mu2, mu2, mu2, mu2, vec(w0), vec(a0), vec(k_k), vec(k_a), vec(ln_w),
      vec(ln_b), vec(r_k), w2, a2, g2, tril, ms, mi, eye)


def _final_kernel(x_ref, g_ref, sh_ref, sc_ref, o_ref):
    o_ref[...] = _rms_mod(x_ref[...], g_ref[...], sh_ref[...], sc_ref[...])


def _final_norm(x2, g, mod5, t_len, tm=256):
    n, d = x2.shape
    tm = min(tm, t_len)
    tpb = t_len // tm
    return pl.pallas_call(
        _final_kernel,
        grid=(n // tm,),
        in_specs=[pl.BlockSpec((tm, d), lambda i: (i, 0)),
                  pl.BlockSpec((1, d), lambda i: (0, 0)),
                  _mod_spec(d, 0, 0, tpb),
                  _mod_spec(d, 0, 1, tpb)],
        out_specs=pl.BlockSpec((tm, d), lambda i: (i, 0)),
        out_shape=jax.ShapeDtypeStruct((n, d), F32),
        compiler_params=_cparams("parallel"),
        name="final_norm",
    )(x2, g.reshape(1, d), mod5, mod5)


def _pad_cols(w, width):
    return jnp.pad(w, ((0, 0), (0, width - w.shape[1])))


def _rwkv_pack_lora(lora_cols, mu, d):
    c1 = RWKV_DECAY_LORA
    c2 = c1 + RWKV_A_LORA
    def pack(a):
        return jnp.concatenate([_pad_cols(a[:, :c1], LANES), _pad_cols(a[:, c1:c2], LANES),
                                a[:, c2:]], axis=-1)
    mu2 = mu.reshape(1, -1)
    return pack(lora_cols), jnp.concatenate([mu2[:, :3 * d], pack(mu2[:, 3 * d:])], axis=-1)[0]


def _pad_rows(w, rows):
    return jnp.pad(w, ((0, rows - w.shape[0]), (0, 0)))


def kernel(x, c, mod_w, mod_b, norm_mix_g, norm_ffn_g, ret_w_in, ret_w_out, rwkv_w_in, rwkv_mu, rwkv_w0, rwkv_w2, rwkv_a0, rwkv_a2, rwkv_g2, rwkv_k_k, rwkv_k_a, rwkv_r_k, rwkv_ln_w, rwkv_ln_b, rwkv_w_out, mlstm_w_in, mlstm_conv_w, mlstm_conv_b, mlstm_gate_b, mlstm_norm_g, mlstm_w_out, ffn_w_up, ffn_conv_w, ffn_conv_b, ffn_w_down, final_g, final_mod_w, final_mod_b):
    n_b, t_len, d = x.shape
    depth = mod_w.shape[0]
    x2 = x.reshape(n_b * t_len, d)

    c_pad = jnp.pad(c, ((0, SUBLANES - n_b), (0, 0)))
    mod5 = _mod_vectors(c_pad, mod_w, mod_b).reshape(depth, SUBLANES, 6, 1, d)
    fmod5 = _mod_vectors(c_pad, final_mod_w[None], final_mod_b[None]).reshape(1, SUBLANES, 2, 1, d)
    mix, ffn = 0, 3

    ret_w_out16 = ret_w_out.astype(BF16)
    rwkv_w_out16 = rwkv_w_out.astype(BF16)
    mlstm_w_out16 = mlstm_w_out.astype(BF16)
    ffn_w_down16 = ffn_w_down.astype(BF16)
    rwkv_w_in16 = rwkv_w_in.astype(BF16)
    mlstm_w_in16 = mlstm_w_in.astype(BF16)

    for i in range(depth):
        kind, j = i % N_MIXERS, i // N_MIXERS
        if kind == 0:
            qkvg = _nm_linear(x2, norm_mix_g[i], mod5, i, mix, ret_w_in, j, t_len, BF16)
            o = _retention(qkvg, n_b, t_len, d)
            x2 = _linear_res(o, ret_w_out16, j, x2, mod5, i, mix + 2, t_len)
        elif kind == 1:
            lora_w, mu = _rwkv_pack_lora(rwkv_w_in16[j, :, 3 * d:], rwkv_mu[j], d)
            p = _nm_linear(x2, norm_mix_g[i], mod5, i, mix, rwkv_w_in16, j, t_len, F32,
                           w_tail=lora_w)
            o = _rwkv_mix(p, mu, rwkv_w0[j], rwkv_a0[j], rwkv_k_k[j], rwkv_k_a[j], rwkv_ln_w[j],
                          rwkv_ln_b[j], rwkv_r_k[j],
                          _pad_rows(rwkv_w2[j], LANES).astype(BF16),
                          _pad_rows(rwkv_a2[j], LANES).astype(BF16),
                          rwkv_g2[j].astype(BF16), n_b, t_len, d)
            x2 = _linear_res(o, rwkv_w_out16, j, x2, mod5, i, mix + 2, t_len)
        else:
            gate_w = _pad_cols(mlstm_w_in16[j, :, 3 * d:], 4 * LANES)
            proj = _nm_linear(x2, norm_mix_g[i], mod5, i, mix, mlstm_w_in16, j, t_len, F32,
                              w_tail=gate_w)
            o = _mlstm(proj, mlstm_conv_w[j], mlstm_conv_b[j], mlstm_gate_b[j], mlstm_norm_g[j],
                       n_b, t_len, d)
            x2 = _linear_res(o, mlstm_w_out16, j, x2, mod5, i, mix + 2, t_len)
        act = _ffn_up(x2, norm_ffn_g[i], mod5, ffn, ffn_w_up, ffn_conv_w, ffn_conv_b, i, t_len)
        x2 = _linear_res(act, ffn_w_down16, i, x2, mod5, i, ffn + 2, t_len)

    out = _final_norm(x2, final_g, fmod5, t_len)
    return out.reshape(n_b, t_len, d)
```

```python
import functools

import numpy as np
import jax
import jax.numpy as jnp
from jax import lax
from jax.experimental import pallas as pl
from jax.experimental.pallas import tpu as pltpu

F32 = jnp.float32
BF16 = jnp.bfloat16

V7X_VMEM_BYTES = 64 * 1024 * 1024
VMEM_LIMIT = (V7X_VMEM_BYTES * 7) // 8
LANES = 128
SUBLANES = 8
BF16_ROWS = 16
NORM_ROWS = 16

EPS = 1e-6
N_MIXERS = 3

RET_HEADS = 8
ROPE_BASE = 10000.0
RET_CHUNK = 128

RWKV_HEAD = 64
RWKV_DECAY_LORA = 96
RWKV_A_LORA = 96
RWKV_GATE_LORA = 256
RWKV_LORA_W = 2 * LANES + RWKV_GATE_LORA
RWKV_LN_EPS = 64e-5
RWKV_CHUNK = 64

MLSTM_HEADS = 4
MLSTM_CONV = 4
MLSTM_CHUNK = 128
GATE_CAP = 15.0

FFN_CONV = 3
NEG_BIG = -1e30


def _cparams(*sem):
    return pltpu.CompilerParams(dimension_semantics=sem, vmem_limit_bytes=VMEM_LIMIT)


def _mm(a, b):
    return jnp.dot(a.astype(BF16), b.astype(BF16), preferred_element_type=F32)


def _mm_nt(a, b):
    return lax.dot_general(a.astype(BF16), b.astype(BF16), (((1,), (1,)), ((), ())),
                           preferred_element_type=F32)


def _split(x):
    hi = x.astype(BF16)
    lo = (x - hi.astype(F32)).astype(BF16)
    return hi, lo


def _mm_split_rhs(m, x):
    hi, lo = _split(x)
    return (jnp.dot(m, hi, preferred_element_type=F32)
            + jnp.dot(m, lo, preferred_element_type=F32))


def _silu(x):
    return x * jax.nn.sigmoid(x)


def _softplus(z):
    return jnp.maximum(z, 0.0) + jnp.log(1.0 + jnp.exp(-jnp.abs(z)))


def _rms_mod1(x, g, shift, scale1):
    y = x * lax.rsqrt(jnp.mean(x * x, axis=-1, keepdims=True) + EPS)
    return (y * g) * scale1 + shift


def _rms_mod(x, g, shift, scale):
    return _rms_mod1(x, g, shift, 1.0 + scale)


def _normalise_rows(x_ref, g_ref, sh_ref, sc_ref, h_ref, row0, rows):
    shape = (NORM_ROWS, x_ref.shape[1])
    g = jnp.broadcast_to(g_ref[...], shape)
    sh = jnp.broadcast_to(sh_ref[...], shape)
    sc1 = jnp.broadcast_to(1.0 + sc_ref[...], shape)
    for r0 in range(0, rows, NORM_ROWS):
        h = _rms_mod1(x_ref[r0:r0 + NORM_ROWS, :], g, sh, sc1)
        h_ref[row0 + r0:row0 + r0 + NORM_ROWS, :] = h.astype(BF16)


def _mod_kernel(c_ref, w_ref, b_ref, o_ref):
    hi, lo = _split(_silu(c_ref[...]))
    w = w_ref[0].astype(BF16)
    o_ref[0] = (jnp.dot(hi, w, preferred_element_type=F32)
                + jnp.dot(lo, w, preferred_element_type=F32)) + b_ref[0]


def _mod_vectors(c_pad, w, b, tn=2048):
    depth, d, n = w.shape
    rows = c_pad.shape[0]
    return pl.pallas_call(
        _mod_kernel,
        grid=(depth, n // tn),
        in_specs=[pl.BlockSpec((rows, d), lambda i, j: (0, 0)),
                  pl.BlockSpec((1, d, tn), lambda i, j: (i, 0, j)),
                  pl.BlockSpec((1, 1, tn), lambda i, j: (i, 0, j))],
        out_specs=pl.BlockSpec((1, rows, tn), lambda i, j: (i, 0, j)),
        out_shape=jax.ShapeDtypeStruct((depth, rows, n), F32),
        compiler_params=_cparams("parallel", "parallel"),
        name="mod_vectors",
    )(c_pad, w, b.reshape(depth, 1, n))


def _mod_spec(d, layer, which, tpb, tn=None):
    if tn is None:
        return pl.BlockSpec((None, None, None, 1, d),
                            lambda i, *_: (layer, i // tpb, which, 0, 0))
    return pl.BlockSpec((None, None, None, 1, tn), lambda i, j: (layer, i // tpb, which, 0, j))


def _nm_linear_kernel(x_ref, g_ref, sh_ref, sc_ref, w_ref, *rest, tm, has_tail):
    wt_ref = rest[0] if has_tail else None
    o_ref, h_ref = rest[-2:]
    col_tile = pl.program_id(1)
    last = pl.num_programs(1) - 1

    @pl.when(col_tile == 0)
    def _():
        _normalise_rows(x_ref, g_ref, sh_ref, sc_ref, h_ref, 0, tm)

    def project(wref):
        o_ref[...] = jnp.dot(h_ref[...], wref[...].astype(BF16),
                             preferred_element_type=F32).astype(o_ref.dtype)

    if has_tail:
        pl.when(col_tile < last)(lambda: project(w_ref))
        pl.when(col_tile == last)(lambda: project(wt_ref))
    else:
        project(w_ref)


def _nm_linear(x2, g, mod5, mlayer, sh_idx, w, wlayer, t_len, out_dtype, w_tail=None, tm=1024):
    n, d = x2.shape
    tn = 1024 if (w_tail is None and w.shape[2] % 1024 == 0) else 512
    nmain = w.shape[2] // tn
    ncol = nmain + (0 if w_tail is None else w_tail.shape[1] // tn)
    tm = min(tm, t_len)
    tpb = t_len // tm
    kern = functools.partial(_nm_linear_kernel, tm=tm, has_tail=w_tail is not None)
    in_specs = [pl.BlockSpec((tm, d), lambda i, j: (i, 0)),
                pl.BlockSpec((1, d), lambda i, j: (0, 0)),
                _mod_spec(d, mlayer, sh_idx, tpb),
                _mod_spec(d, mlayer, sh_idx + 1, tpb),
                pl.BlockSpec((None, d, tn), lambda i, j: (wlayer, 0, jnp.minimum(j, nmain - 1)))]
    args = [x2, g.reshape(1, d), mod5, mod5, w]
    if w_tail is not None:
        in_specs.append(pl.BlockSpec((d, tn), lambda i, j: (0, jnp.maximum(j - nmain, 0))))
        args.append(w_tail)
    return pl.pallas_call(
        kern,
        grid=(n // tm, ncol),
        in_specs=in_specs,
        out_specs=pl.BlockSpec((tm, tn), lambda i, j: (i, j)),
        out_shape=jax.ShapeDtypeStruct((n, ncol * tn), out_dtype),
        scratch_shapes=[pltpu.VMEM((tm, d), BF16)],
        compiler_params=_cparams("parallel", "arbitrary"),
        name="norm_mod_linear",
    )(*args)


def _ffn_up_kernel(x_ref, xp_ref, g_ref, sh_ref, sc_ref, wg_ref, wv_ref, cw_ref, cb_ref,
                   o_ref, h_ref, ext_ref, *, tm, rchunk, tpb):
    first = (pl.program_id(0) % tpb) == 0
    col_tile = pl.program_id(1)
    halo = BF16_ROWS

    @pl.when(col_tile == 0)
    def _():
        _normalise_rows(xp_ref, g_ref, sh_ref, sc_ref, h_ref, 0, halo)
        _normalise_rows(x_ref, g_ref, sh_ref, sc_ref, h_ref, halo, tm)

    ext_ref[...] = jnp.dot(h_ref[...], wg_ref[...].astype(BF16), preferred_element_type=F32)
    ext_ref[0:halo, :] = jnp.where(first, 0.0, ext_ref[0:halo, :])
    val = jnp.dot(h_ref[halo:halo + tm, :], wv_ref[...].astype(BF16),
                  preferred_element_type=F32)
    for r0 in range(0, tm, rchunk):
        base = halo + r0
        conv = (cw_ref[0:1, :] * ext_ref[base - 2:base - 2 + rchunk, :]
                + cw_ref[1:2, :] * ext_ref[base - 1:base - 1 + rchunk, :]
                + cw_ref[2:3, :] * ext_ref[base:base + rchunk, :]
                + cb_ref[...])
        o_ref[r0:r0 + rchunk, :] = (_silu(conv) * val[r0:r0 + rchunk, :]).astype(o_ref.dtype)


def _ffn_up(x2, g, mod5, sh_idx, w_up, conv_w, conv_b, layer, t_len, tm=1024, tn=512):
    n, d = x2.shape
    f = w_up.shape[2] // 2
    tm = min(tm, t_len)
    tpb = t_len // tm
    rb = tm // BF16_ROWS
    nf = f // tn
    kern = functools.partial(_ffn_up_kernel, tm=tm, rchunk=min(256, tm), tpb=tpb)
    return pl.pallas_call(
        kern,
        grid=(n // tm, nf),
        in_specs=[pl.BlockSpec((tm, d), lambda i, j: (i, 0)),
                  pl.BlockSpec((BF16_ROWS, d), lambda i, j: (jnp.maximum(i * rb - 1, 0), 0)),
                  pl.BlockSpec((1, d), lambda i, j: (0, 0)),
                  _mod_spec(d, layer, sh_idx, tpb),
                  _mod_spec(d, layer, sh_idx + 1, tpb),
                  pl.BlockSpec((None, d, tn), lambda i, j: (layer, 0, j)),
                  pl.BlockSpec((None, d, tn), lambda i, j: (layer, 0, nf + j)),
                  pl.BlockSpec((None, FFN_CONV, tn), lambda i, j: (layer, 0, j)),
                  pl.BlockSpec((None, 1, tn), lambda i, j: (layer, 0, j))],
        out_specs=pl.BlockSpec((tm, tn), lambda i, j: (i, j)),
        out_shape=jax.ShapeDtypeStruct((n, f), BF16),
        scratch_shapes=[pltpu.VMEM((tm + BF16_ROWS, d), BF16),
                        pltpu.VMEM((tm + BF16_ROWS, tn), F32)],
        compiler_params=_cparams("parallel", "arbitrary"),
        name="ffn_up_act",
    )(x2, x2, g.reshape(1, d), mod5, mod5, w_up, w_up, conv_w,
      conv_b.reshape(conv_b.shape[0], 1, f))


def _linear_res_kernel(a_ref, w_ref, x_ref, gate_ref, o_ref):
    acc = jnp.dot(a_ref[...], w_ref[...], preferred_element_type=F32)
    o_ref[...] = x_ref[...] + gate_ref[...] * acc


def _linear_res(a, w, wlayer, x2, mod5, mlayer, gate_idx, t_len, tm=1024):
    n, k = a.shape
    m = w.shape[2]
    tn = 1024 if k <= 2048 else 512
    tm = min(tm, t_len)
    tpb = t_len // tm
    return pl.pallas_call(
        _linear_res_kernel,
        grid=(n // tm, m // tn),
        in_specs=[pl.BlockSpec((tm, k), lambda i, j: (i, 0)),
                  pl.BlockSpec((None, k, tn), lambda i, j: (wlayer, 0, j)),
                  pl.BlockSpec((tm, tn), lambda i, j: (i, j)),
                  _mod_spec(m, mlayer, gate_idx, tpb, tn)],
        out_specs=pl.BlockSpec((tm, tn), lambda i, j: (i, j)),
        out_shape=jax.ShapeDtypeStruct((n, m), F32),
        compiler_params=_cparams("parallel", "arbitrary"),
        name="linear_residual",
    )(a, w, x2, mod5)


def _retention_kernel(q_ref, k_ref, v_ref, g_ref, cos_ref, sin_ref, dint_ref, dq_ref, dk_ref,
                      dc_ref, o_ref, s_ref, *, heads, dk, dv):
    chunk = pl.program_id(1)

    @pl.when(chunk == 0)
    def _():
        s_ref[...] = jnp.zeros_like(s_ref)

    cos = cos_ref[...]
    sin = sin_ref[...]
    half = dk // 2

    def rot(ref, h):
        x1 = ref[:, h * dk:h * dk + half].astype(F32)
        x2 = ref[:, h * dk + half:(h + 1) * dk].astype(F32)
        return jnp.concatenate([x1 * cos - x2 * sin, x1 * sin + x2 * cos], axis=-1)

    hs = range(heads)
    q = [rot(q_ref, h) for h in hs]
    k = [rot(k_ref, h) * (dk ** -0.5) for h in hs]
    v = [v_ref[:, h * dv:(h + 1) * dv] for h in hs]
    scores = [_mm_nt(q[h], k[h]) * dint_ref[h] for h in hs]
    state = [s_ref[h] for h in hs]
    out = [_mm(scores[h], v[h]) + _mm(q[h] * dq_ref[h], state[h]) for h in hs]
    for h in hs:
        s_ref[h] = dc_ref[h] * state[h] + _mm((k[h] * dk_ref[h]).T, v[h])
    for h in hs:
        oc = out[h] - jnp.mean(out[h], axis=-1, keepdims=True)
        o = oc * lax.rsqrt(jnp.mean(oc * oc, axis=-1, keepdims=True) + EPS)
        gate = g_ref[:, h * dv:(h + 1) * dv].astype(F32)
        o_ref[:, h * dv:(h + 1) * dv] = (o * _silu(gate)).astype(o_ref.dtype)


def _retention_tables(t_len, dk):
    half = dk // 2
    inv = ROPE_BASE ** (-np.arange(half, dtype=np.float64) / half)
    ang = np.arange(t_len, dtype=np.float64)[:, None] * inv[None, :]
    lg = np.log1p(-np.power(2.0, -5.0 - np.arange(RET_HEADS, dtype=np.float64)))
    idx = np.arange(RET_CHUNK, dtype=np.float64)
    diff = idx[:, None] - idx[None, :]
    dint = np.where(diff >= 0, np.exp(lg[:, None, None] * np.maximum(diff, 0.0)), 0.0)
    dq = np.exp(lg[:, None] * (idx + 1.0))[..., None]
    dkk = np.exp(lg[:, None] * (RET_CHUNK - 1.0 - idx))[..., None]
    dc = np.exp(lg * RET_CHUNK)[:, None, None]
    f = lambda a: jnp.asarray(a, F32)
    return f(np.cos(ang)), f(np.sin(ang)), f(dint), f(dq), f(dkk), f(dc)


def _retention(qkvg, n_b, t_len, d):
    n = qkvg.shape[0]
    heads = RET_HEADS
    dk = d // heads
    dv = 2 * dk
    lc = RET_CHUNK
    nc = t_len // lc
    cos, sin, dint, dq, dkk, dc = _retention_tables(t_len, dk)
    row = lambda b, c: b * nc + c
    full = lambda a: pl.BlockSpec(a.shape, lambda b, c: (0,) * a.ndim)
    kern = functools.partial(_retention_kernel, heads=heads, dk=dk, dv=dv)
    return pl.pallas_call(
        kern,
        grid=(n_b, nc),
        in_specs=[pl.BlockSpec((lc, d), lambda b, c: (row(b, c), 0)),
                  pl.BlockSpec((lc, d), lambda b, c: (row(b, c), 1)),
                  pl.BlockSpec((lc, 2 * d), lambda b, c: (row(b, c), 1)),
                  pl.BlockSpec((lc, 2 * d), lambda b, c: (row(b, c), 2)),
                  pl.BlockSpec((lc, dk // 2), lambda b, c: (c, 0)),
                  pl.BlockSpec((lc, dk // 2), lambda b, c: (c, 0)),
                  full(dint), full(dq), full(dkk), full(dc)],
        out_specs=pl.BlockSpec((lc, 2 * d), lambda b, c: (row(b, c), 0)),
        out_shape=jax.ShapeDtypeStruct((n, 2 * d), BF16),
        scratch_shapes=[pltpu.VMEM((heads, dk, dv), F32)],
        compiler_params=_cparams("parallel", "arbitrary"),
        name="retention_chunk",
    )(qkvg, qkvg, qkvg, qkvg, cos, sin, dint, dq, dkk, dc)


def _mlstm_kernel(q_ref, k_ref, v_ref, og_ref, gt_ref, cw_ref, cb_ref, gb_ref, ng_ref, tril_ref,
                  o_ref, qext, kext, c_ref, n_ref, m_ref, *, lc, dqk, dv, heads):
    chunk = pl.program_id(1)
    qw = heads * dqk

    @pl.when(chunk == 0)
    def _():
        qext[0:SUBLANES, :] = jnp.zeros((SUBLANES, qw), F32)
        kext[0:SUBLANES, :] = jnp.zeros((SUBLANES, qw), F32)
        c_ref[...] = jnp.zeros_like(c_ref)
        n_ref[...] = jnp.zeros_like(n_ref)
        m_ref[...] = jnp.zeros_like(m_ref)

    def conv_silu(raw_ref, ext, c0):
        ext[SUBLANES:SUBLANES + lc, :] = raw_ref[...]
        y = cb_ref[:, c0:c0 + qw]
        for j in range(MLSTM_CONV):
            off = SUBLANES - (MLSTM_CONV - 1) + j
            y = y + cw_ref[j:j + 1, c0:c0 + qw] * ext[off:off + lc, :]
        ext[0:SUBLANES, :] = ext[lc:lc + SUBLANES, :]
        return _silu(y)

    q_all = conv_silu(q_ref, qext, 0)
    k_all = conv_silu(k_ref, kext, qw) * (dqk ** -0.5)

    gt = gt_ref[...] + gb_ref[...]
    lane = lax.broadcasted_iota(jnp.int32, gt.shape, 1)
    tril = tril_ref[...]
    causal = (lax.broadcasted_iota(jnp.int32, (lc, lc), 0)
              >= lax.broadcasted_iota(jnp.int32, (lc, lc), 1))

    hs = range(heads)
    q = [q_all[:, h * dqk:(h + 1) * dqk] for h in hs]
    k = [k_all[:, h * dqk:(h + 1) * dqk] for h in hs]
    v = [v_ref[:, h * dv:(h + 1) * dv] for h in hs]
    ipre = [jnp.sum(jnp.where(lane == h, gt, 0.0), axis=-1, keepdims=True) for h in hs]
    fpre = [jnp.sum(jnp.where(lane == h + heads, gt, 0.0), axis=-1, keepdims=True) for h in hs]
    li = [GATE_CAP * jnp.tanh(x / GATE_CAP) for x in ipre]
    lf = [-_softplus(-(GATE_CAP * jnp.tanh(x / GATE_CAP))) for x in fpre]
    b_full = [_mm_split_rhs(tril, jnp.broadcast_to(x, (lc, lc))) for x in lf]
    b_col = [x[:, 0:1] for x in b_full]
    dmat = [jnp.where(causal, b_full[h] - b_full[h].T + jnp.broadcast_to(li[h], (lc, lc)).T,
                      NEG_BIG) for h in hs]
    m_st = [m_ref[h] for h in hs]
    m_inter = [b_col[h] + m_st[h] for h in hs]
    m_t = [jnp.maximum(m_inter[h], jnp.max(dmat[h], axis=-1, keepdims=True)) for h in hs]
    scores = [_mm_nt(q[h], k[h]) * jnp.exp(dmat[h] - m_t[h]) for h in hs]
    w_inter = [jnp.exp(m_inter[h] - m_t[h]) for h in hs]
    c_st = [c_ref[h] for h in hs]
    n_st = [n_ref[h] for h in hs]
    num = [_mm(scores[h], v[h]) + w_inter[h] * _mm(q[h], c_st[h]) for h in hs]
    den = [jnp.sum(scores[h], axis=-1, keepdims=True)
           + w_inter[h] * jnp.sum(q[h] * n_st[h], axis=-1, keepdims=True) for h in hs]
    for h in hs:
        b_last = b_col[h][lc - 1:lc, :]
        gdec = b_last - b_col[h] + li[h]
        m_new = jnp.maximum(b_last + m_st[h], jnp.max(gdec, axis=0, keepdims=True))
        wk = jnp.exp(gdec - m_new)
        carry = jnp.exp(b_last + m_st[h] - m_new)
        kw = k[h] * wk
        c_ref[h] = carry * c_st[h] + _mm(kw.T, v[h])
        n_ref[h] = carry * n_st[h] + jnp.sum(kw, axis=0, keepdims=True)
        m_ref[h] = m_new
    for h in hs:
        hh = num[h] / jnp.maximum(jnp.abs(den[h]), jnp.exp(-m_t[h]))
        cs = slice(h * dv, (h + 1) * dv)
        hn = hh * lax.rsqrt(jnp.mean(hh * hh, axis=-1, keepdims=True) + EPS) * ng_ref[:, cs]
        o_ref[:, cs] = (hn * jax.nn.sigmoid(og_ref[:, cs])).astype(o_ref.dtype)


def _mlstm(proj, conv_w, conv_b, gate_b, norm_g, n_b, t_len, d):
    n = proj.shape[0]
    heads = MLSTM_HEADS
    dqk = d // (2 * heads)
    dv = d // heads
    lc = MLSTM_CHUNK
    nc = t_len // lc
    qw = heads * dqk
    gtb = (2 * qw + 2 * d) // LANES
    gb = jnp.zeros((1, LANES), F32).at[0, :2 * heads].set(gate_b.astype(F32))
    tril = jnp.asarray(np.tril(np.ones((lc, lc), np.float32)), BF16)
    kern = functools.partial(_mlstm_kernel, lc=lc, dqk=dqk, dv=dv, heads=heads)
    row = lambda b, c: b * nc + c
    full = lambda a: pl.BlockSpec(a.shape, lambda b, c: (0,) * a.ndim)
    cb = conv_b.reshape(1, 2 * qw)
    ng = norm_g.reshape(1, d)
    return pl.pallas_call(
        kern,
        grid=(n_b, nc),
        in_specs=[pl.BlockSpec((lc, qw), lambda b, c: (row(b, c), 0)),
                  pl.BlockSpec((lc, qw), lambda b, c: (row(b, c), 1)),
                  pl.BlockSpec((lc, d), lambda b, c: (row(b, c), (2 * qw) // d)),
                  pl.BlockSpec((lc, d), lambda b, c: (row(b, c), (2 * qw) // d + 1)),
                  pl.BlockSpec((lc, LANES), lambda b, c: (row(b, c), gtb)),
                  full(conv_w), full(cb), full(gb), full(ng), full(tril)],
        out_specs=pl.BlockSpec((lc, d), lambda b, c: (row(b, c), 0)),
        out_shape=jax.ShapeDtypeStruct((n, d), BF16),
        scratch_shapes=[pltpu.VMEM((lc + SUBLANES, qw), F32),
                        pltpu.VMEM((lc + SUBLANES, qw), F32),
                        pltpu.VMEM((heads, dqk, dv), F32),
                        pltpu.VMEM((heads, 1, dqk), F32),
                        pltpu.VMEM((heads, 1, 1), F32)],
        compiler_params=_cparams("parallel", "arbitrary"),
        name="mlstm_chunk",
    )(proj, proj, proj, proj, proj, conv_w, cb, gb, ng, tril)


def _rwkv_kernel(pr_ref, pk_ref, pv_ref, pl_ref, mur_ref, muk_ref, muv_ref, mul_ref,
                 w0_ref, a0_ref, kk_ref, ka_ref, lnw_ref, lnb_ref, rk_ref,
                 w2_ref, a2_ref, g2_ref, tril_ref, ms_ref, mi_ref, eye_ref,
                 o_ref, h_ref, prev_ref, ext_ref, y_ref, *, lc, nchunk, npp):
    block = pl.program_id(2)
    lb = lc * nchunk
    pw = npp * LANES

    @pl.when(block == 0)
    def _():
        h_ref[...] = jnp.zeros_like(h_ref)
        prev_ref[...] = jnp.zeros_like(prev_ref)

    lane = lax.broadcasted_iota(jnp.int32, (lb, LANES), 1)
    head0 = lane < RWKV_HEAD
    head0_c = lax.broadcasted_iota(jnp.int32, (lc, LANES), 1) < RWKV_HEAD

    def pair_sum(x):
        s0 = jnp.sum(jnp.where(head0, x, 0.0), axis=-1, keepdims=True)
        s1 = jnp.sum(jnp.where(head0, 0.0, x), axis=-1, keepdims=True)
        return jnp.where(head0, s0, s1)

    def shifted(src_ref, mu_ref, c0, pc0):
        cs = slice(c0, c0 + LANES)
        ext_ref[0:SUBLANES, :] = prev_ref[:, pc0:pc0 + LANES]
        ext_ref[SUBLANES:SUBLANES + lb, :] = src_ref[:, cs]
        cur = src_ref[:, cs]
        return cur + mu_ref[:, cs] * (ext_ref[SUBLANES - 1:SUBLANES - 1 + lb, :] - cur)

    lo0 = 3 * pw
    wlo = jnp.tanh(shifted(pl_ref, mul_ref, 0, lo0)).astype(BF16)
    alo = shifted(pl_ref, mul_ref, LANES, lo0 + LANES).astype(BF16)
    glo = jax.nn.sigmoid(jnp.concatenate(
        [shifted(pl_ref, mul_ref, 2 * LANES + s, lo0 + 2 * LANES + s)
         for s in range(0, RWKV_GATE_LORA, LANES)], axis=-1)).astype(BF16)

    r_p, k_p, v_p, kn_p, bb_p, wl_p, g_p = [], [], [], [], [], [], []
    for pi in range(npp):
        cs = slice(pi * LANES, (pi + 1) * LANES)
        r = shifted(pr_ref, mur_ref, pi * LANES, pi * LANES)
        k = shifted(pk_ref, muk_ref, pi * LANES, pw + pi * LANES)
        v = shifted(pv_ref, muv_ref, pi * LANES, 2 * pw + pi * LANES)
        wraw = w0_ref[:, cs] + jnp.dot(wlo, w2_ref[:, cs], preferred_element_type=F32)
        ag = jax.nn.sigmoid(a0_ref[:, cs] + jnp.dot(alo, a2_ref[:, cs],
                                                    preferred_element_type=F32))
        kn = k * kk_ref[:, cs]
        kn = kn / jnp.maximum(jnp.sqrt(pair_sum(kn * kn)), 1e-12)
        r_p.append(r)
        k_p.append(k * (1.0 + (ag - 1.0) * ka_ref[:, cs]))
        v_p.append(v)
        kn_p.append(kn)
        bb_p.append(kn * ag)
        wl_p.append(-jnp.exp(-_softplus(-wraw) - 0.5))
        g_p.append(jnp.dot(glo, g2_ref[:, cs], preferred_element_type=F32))

    tail = slice(lb - SUBLANES, lb)
    prev_ref[:, 0:pw] = pr_ref[tail, :]
    prev_ref[:, pw:2 * pw] = pk_ref[tail, :]
    prev_ref[:, 2 * pw:3 * pw] = pv_ref[tail, :]
    prev_ref[:, 3 * pw:3 * pw + RWKV_LORA_W] = pl_ref[tail, :]

    def stack(x):
        return jnp.concatenate([jnp.where(head0_c, x, 0.0), jnp.where(head0_c, 0.0, x)], axis=0)

    tril = tril_ref[...]
    ms = ms_ref[...]
    mi = mi_ref[...]
    eye = eye_ref[...]
    two = 2 * lc
    probs = [(pi, ci) for ci in range(nchunk) for pi in range(npp)]
    ps = range(len(probs))

    def rows(vals):
        return [vals[pi][ci * lc:(ci + 1) * lc, :] for pi, ci in probs]

    wl = rows(wl_p)
    cum = [_mm_split_rhs(tril, x) for x in wl]
    cl = [x[lc - 1:lc, :] for x in cum]
    r, k, v, kn, bb = rows(r_p), rows(k_p), rows(v_p), rows(kn_p), rows(bb_p)
    e_n = [jnp.exp(-cum[i]) for i in ps]
    e_l = [jnp.exp(cl[i] - cum[i]) for i in ps]
    a_s = [stack(-kn[i] * jnp.exp(cum[i] - wl[i])) for i in ps]
    r_s = [stack(r[i] * jnp.exp(cum[i])) for i in ps]
    b_s = [stack(bb[i] * e_n[i]) for i in ps]
    k_s = [stack(k[i] * e_n[i]) for i in ps]
    v_s = [stack(v[i]) for i in ps]
    bp_t = [stack(bb[i] * e_l[i]).T for i in ps]
    kp_t = [stack(k[i] * e_l[i]).T for i in ps]

    gm = [_mm_nt(jnp.concatenate([a_s[i], r_s[i]], axis=0),
                 jnp.concatenate([b_s[i], k_s[i]], axis=0)) for i in ps]
    npow = [gm[i][:two, :two] * ms for i in ps]
    aak_v = [_mm(gm[i][:two, two:] * ms, v_s[i]) for i in ps]
    arb = [gm[i][two:, :two] * mi for i in ps]
    ark_v = [_mm(gm[i][two:, two:] * mi, v_s[i]) for i in ps]
    kp_v = [_mm(kp_t[i], v_s[i]) for i in ps]

    tinv = [eye + npow[i] for i in ps]
    for _ in range(int(np.log2(lc)) - 1):
        npow = [_mm(npow[i], npow[i]) for i in ps]
        tinv = [_mm(tinv[i], eye + npow[i]) for i in ps]

    taw = [_mm(tinv[i], jnp.concatenate([a_s[i], aak_v[i]], axis=1)) for i in ps]
    arb_taw = [_mm(arb[i], taw[i]) for i in ps]
    bp_taw = [_mm(bp_t[i], taw[i]) for i in ps]
    qh = [r_s[i] + arb_taw[i][:, :LANES] for i in ps]
    yin = [arb_taw[i][:, LANES:] + ark_v[i] for i in ps]
    phi = [eye * jnp.exp(cl[i]) + bp_taw[i][:, :LANES] for i in ps]
    psi = [bp_taw[i][:, LANES:] + kp_v[i] for i in ps]

    hst = [h_ref[pi] for pi in range(npp)]
    for i, (pi, ci) in enumerate(probs):
        ys = _mm(qh[i], hst[pi]) + yin[i]
        hst[pi] = _mm(phi[i], hst[pi]) + psi[i]
        y_ref[pi, ci * lc:(ci + 1) * lc, :] = ys[:lc] + ys[lc:]
    for pi in range(npp):
        h_ref[pi] = hst[pi]

    inv_head = 1.0 / RWKV_HEAD
    for pi in range(npp):
        cs = slice(pi * LANES, (pi + 1) * LANES)
        y = y_ref[pi]
        yc = y - pair_sum(y) * inv_head
        var = pair_sum(yc * yc) * inv_head
        yn = yc * lax.rsqrt(var + RWKV_LN_EPS) * lnw_ref[:, cs] + lnb_ref[:, cs]
        bonus = pair_sum(r_p[pi] * k_p[pi] * rk_ref[:, cs])
        o_ref[:, cs] = ((yn + bonus * v_p[pi]) * g_p[pi]).astype(o_ref.dtype)


def _rwkv_mix(p, mu, w0, a0, k_k, k_a, ln_w, ln_b, r_k, w2, a2, g2, n_b, t_len, d,
              nchunk=4, npp=4):
    lc = RWKV_CHUNK
    lb = lc * nchunk
    nb = t_len // lb
    pw = npp * LANES
    ngrp = d // pw
    two = 2 * lc
    lora_blk = (3 * d) // RWKV_LORA_W
    idx = np.arange(two)
    same = (idx[:, None] // lc) == (idx[None, :] // lc)
    ms = jnp.asarray((same & (idx[:, None] > idx[None, :])).astype(np.float32))
    mi = jnp.asarray((same & (idx[:, None] >= idx[None, :])).astype(np.float32))
    eye = jnp.asarray(np.eye(two, dtype=np.float32))
    tril = jnp.asarray(np.tril(np.ones((lc, lc), np.float32)), BF16)
    row = lambda b, t: b * nb + t
    pspec = lambda off: pl.BlockSpec((lb, pw), lambda b, g, t: (row(b, t), off + g))
    mspec = lambda off: pl.BlockSpec((1, pw), lambda b, g, t: (0, off + g))
    vspec = pl.BlockSpec((1, pw), lambda b, g, t: (0, g))
    wspec = lambda rows: pl.BlockSpec((rows, pw), lambda b, g, t: (0, g))
    sq = lambda s: pl.BlockSpec((s, s), lambda b, g, t: (0, 0))
    vec = lambda a: a.reshape(1, -1)
    mu2 = vec(mu)
    kern = functools.partial(_rwkv_kernel, lc=lc, nchunk=nchunk, npp=npp)
    return pl.pallas_call(
        kern,
        grid=(n_b, ngrp, nb),
        in_specs=[pspec(0), pspec(ngrp), pspec(2 * ngrp),
                  pl.BlockSpec((lb, RWKV_LORA_W), lambda b, g, t: (row(b, t), lora_blk)),
                  mspec(0), mspec(ngrp), mspec(2 * ngrp),
                  pl.BlockSpec((1, RWKV_LORA_W), lambda b, g, t: (0, lora_blk)),
                  vspec, vspec, vspec, vspec, vspec, vspec, vspec,
                  wspec(LANES), wspec(LANES), wspec(RWKV_GATE_LORA),
                  sq(lc), sq(two), sq(two), sq(two)],
        out_specs=pl.BlockSpec((lb, pw), lambda b, g, t: (row(b, t), g)),
        out_shape=jax.ShapeDtypeStruct((n_b * t_len, d), BF16),
        scratch_shapes=[pltpu.VMEM((npp, LANES, LANES), F32),
                        pltpu.VMEM((SUBLANES, 3 * pw + RWKV_LORA_W), F32),
                        pltpu.VMEM((lb + SUBLANES, LANES), F32),
                        pltpu.VMEM((npp, lb, LANES), F32)],
        compiler_params=_cparams("parallel", "parallel", "arbitrary"),
        name="rwkv_mix",
    )(p, p, p, p, mu2, mu2, mu2, mu2, vec(w0), vec(a0), vec(k_k), vec(k_a), vec(ln_w),
      vec(ln_b), vec(r_k), w2, a2, g2, tril, ms, mi, eye)


def _final_kernel(x_ref, g_ref, sh_ref, sc_ref, o_ref):
    o_ref[...] = _rms_mod(x_ref[...], g_ref[...], sh_ref[...], sc_ref[...])


def _final_norm(x2, g, mod5, t_len, tm=256):
    n, d = x2.shape
    tm = min(tm, t_len)
    tpb = t_len // tm
    return pl.pallas_call(
        _final_kernel,
        grid=(n // tm,),
        in_specs=[pl.BlockSpec((tm, d), lambda i: (i, 0)),
                  pl.BlockSpec((1, d), lambda i: (0, 0)),
                  _mod_spec(d, 0, 0, tpb),
                  _mod_spec(d, 0, 1, tpb)],
        out_specs=pl.BlockSpec((tm, d), lambda i: (i, 0)),
        out_shape=jax.ShapeDtypeStruct((n, d), F32),
        compiler_params=_cparams("parallel"),
        name="final_norm",
    )(x2, g.reshape(1, d), mod5, mod5)


def _pad_cols(w, width):
    return jnp.pad(w, ((0, 0), (0, width - w.shape[1])))


def _rwkv_pack_lora(lora_cols, mu, d):
    c1 = RWKV_DECAY_LORA
    c2 = c1 + RWKV_A_LORA
    def pack(a):
        return jnp.concatenate([_pad_cols(a[:, :c1], LANES), _pad_cols(a[:, c1:c2], LANES),
                                a[:, c2:]], axis=-1)
    mu2 = mu.reshape(1, -1)
    return pack(lora_cols), jnp.concatenate([mu2[:, :3 * d], pack(mu2[:, 3 * d:])], axis=-1)[0]


def _pad_rows(w, rows):
    return jnp.pad(w, ((0, rows - w.shape[0]), (0, 0)))


def kernel(x, c, mod_w, mod_b, norm_mix_g, norm_ffn_g, ret_w_in, ret_w_out, rwkv_w_in, rwkv_mu, rwkv_w0, rwkv_w2, rwkv_a0, rwkv_a2, rwkv_g2, rwkv_k_k, rwkv_k_a, rwkv_r_k, rwkv_ln_w, rwkv_ln_b, rwkv_w_out, mlstm_w_in, mlstm_conv_w, mlstm_conv_b, mlstm_gate_b, mlstm_norm_g, mlstm_w_out, ffn_w_up, ffn_conv_w, ffn_conv_b, ffn_w_down, final_g, final_mod_w, final_mod_b):
    n_b, t_len, d = x.shape
    depth = mod_w.shape[0]
    x2 = x.reshape(n_b * t_len, d)

    c_pad = jnp.pad(c, ((0, SUBLANES - n_b), (0, 0)))
    mod5 = _mod_vectors(c_pad, mod_w, mod_b).reshape(depth, SUBLANES, 6, 1, d)
    fmod5 = _mod_vectors(c_pad, final_mod_w[None], final_mod_b[None]).reshape(1, SUBLANES, 2, 1, d)
    mix, ffn = 0, 3

    ret_w_out16 = ret_w_out.astype(BF16)
    rwkv_w_out16 = rwkv_w_out.astype(BF16)
    mlstm_w_out16 = mlstm_w_out.astype(BF16)
    ffn_w_down16 = ffn_w_down.astype(BF16)
    rwkv_w_in16 = rwkv_w_in.astype(BF16)
    mlstm_w_in16 = mlstm_w_in.astype(BF16)

    for i in range(depth):
        kind, j = i % N_MIXERS, i // N_MIXERS
        if kind == 0:
            qkvg = _nm_linear(x2, norm_mix_g[i], mod5, i, mix, ret_w_in, j, t_len, BF16)
            o = _retention(qkvg, n_b, t_len, d)
            x2 = _linear_res(o, ret_w_out16, j, x2, mod5, i, mix + 2, t_len)
        elif kind == 1:
            lora_w, mu = _rwkv_pack_lora(rwkv_w_in16[j, :, 3 * d:], rwkv_mu[j], d)
            p = _nm_linear(x2, norm_mix_g[i], mod5, i, mix, rwkv_w_in16, j, t_len, F32,
                           w_tail=lora_w)
            o = _rwkv_mix(p, mu, rwkv_w0[j], rwkv_a0[j], rwkv_k_k[j], rwkv_k_a[j], rwkv_ln_w[j],
                          rwkv_ln_b[j], rwkv_r_k[j],
                          _pad_rows(rwkv_w2[j], LANES).astype(BF16),
                          _pad_rows(rwkv_a2[j], LANES).astype(BF16),
                          rwkv_g2[j].astype(BF16), n_b, t_len, d)
            x2 = _linear_res(o, rwkv_w_out16, j, x2, mod5, i, mix + 2, t_len)
        else:
            gate_w = _pad_cols(mlstm_w_in16[j, :, 3 * d:], 4 * LANES)
            proj = _nm_linear(x2, norm_mix_g[i], mod5, i, mix, mlstm_w_in16, j, t_len, F32,
                              w_tail=gate_w)
            o = _mlstm(proj, mlstm_conv_w[j], mlstm_conv_b[j], mlstm_gate_b[j], mlstm_norm_g[j],
                       n_b, t_len, d)
            x2 = _linear_res(o, mlstm_w_out16, j, x2, mod5, i, mix + 2, t_len)
        act = _ffn_up(x2, norm_ffn_g[i], mod5, ffn, ffn_w_up, ffn_conv_w, ffn_conv_b, i, t_len)
        x2 = _linear_res(act, ffn_w_down16, i, x2, mod5, i, ffn + 2, t_len)

    out = _final_norm(x2, final_g, fmod5, t_len)
    return out.reshape(n_b, t_len, d)
```

```python
import functools

import numpy as np
import jax
import jax.numpy as jnp
from jax import lax
from jax.experimental import pallas as pl
from jax.experimental.pallas import tpu as pltpu

F32 = jnp.float32
BF16 = jnp.bfloat16

V7X_VMEM_BYTES = 64 * 1024 * 1024
VMEM_LIMIT = (V7X_VMEM_BYTES * 7) // 8
LANES = 128
SUBLANES = 8
BF16_ROWS = 16
NORM_ROWS = 16

EPS = 1e-6
N_MIXERS = 3

RET_HEADS = 8
ROPE_BASE = 10000.0
RET_CHUNK = 256

RWKV_HEAD = 64
RWKV_DECAY_LORA = 96
RWKV_A_LORA = 96
RWKV_GATE_LORA = 256
RWKV_LORA_W = 2 * LANES + RWKV_GATE_LORA
RWKV_LN_EPS = 64e-5
RWKV_CHUNK = 64

MLSTM_HEADS = 4
MLSTM_CONV = 4
MLSTM_CHUNK = 128
GATE_CAP = 15.0

FFN_CONV = 3
NEG_BIG = -1e30


def _cparams(*sem):
    return pltpu.CompilerParams(dimension_semantics=sem, vmem_limit_bytes=VMEM_LIMIT)


def _mm(a, b):
    return jnp.dot(a.astype(BF16), b.astype(BF16), preferred_element_type=F32)


def _mm_nt(a, b):
    return lax.dot_general(a.astype(BF16), b.astype(BF16), (((1,), (1,)), ((), ())),
                           preferred_element_type=F32)


def _split(x):
    hi = x.astype(BF16)
    lo = (x - hi.astype(F32)).astype(BF16)
    return hi, lo


def _mm_split_rhs(m, x):
    hi, lo = _split(x)
    return (jnp.dot(m, hi, preferred_element_type=F32)
            + jnp.dot(m, lo, preferred_element_type=F32))


def _silu(x):
    return x * jax.nn.sigmoid(x)


def _softplus(z):
    return jnp.maximum(z, 0.0) + jnp.log(1.0 + jnp.exp(-jnp.abs(z)))


def _rms_mod1(x, g, shift, scale1):
    y = x * lax.rsqrt(jnp.mean(x * x, axis=-1, keepdims=True) + EPS)
    return (y * g) * scale1 + shift


def _rms_mod(x, g, shift, scale):
    return _rms_mod1(x, g, shift, 1.0 + scale)


def _normalise_rows(x_ref, g_ref, sh_ref, sc_ref, h_ref, row0, rows):
    shape = (NORM_ROWS, x_ref.shape[1])
    g = jnp.broadcast_to(g_ref[...], shape)
    sh = jnp.broadcast_to(sh_ref[...], shape)
    sc1 = jnp.broadcast_to(1.0 + sc_ref[...], shape)
    for r0 in range(0, rows, NORM_ROWS):
        h = _rms_mod1(x_ref[r0:r0 + NORM_ROWS, :], g, sh, sc1)
        h_ref[row0 + r0:row0 + r0 + NORM_ROWS, :] = h.astype(BF16)


def _mod_kernel(c_ref, w_ref, b_ref, o_ref):
    hi, lo = _split(_silu(c_ref[...]))
    w = w_ref[0].astype(BF16)
    o_ref[0] = (jnp.dot(hi, w, preferred_element_type=F32)
                + jnp.dot(lo, w, preferred_element_type=F32)) + b_ref[0]


def _mod_vectors(c_pad, w, b, tn=2048):
    depth, d, n = w.shape
    rows = c_pad.shape[0]
    return pl.pallas_call(
        _mod_kernel,
        grid=(depth, n // tn),
        in_specs=[pl.BlockSpec((rows, d), lambda i, j: (0, 0)),
                  pl.BlockSpec((1, d, tn), lambda i, j: (i, 0, j)),
                  pl.BlockSpec((1, 1, tn), lambda i, j: (i, 0, j))],
        out_specs=pl.BlockSpec((1, rows, tn), lambda i, j: (i, 0, j)),
        out_shape=jax.ShapeDtypeStruct((depth, rows, n), F32),
        compiler_params=_cparams("parallel", "parallel"),
        name="mod_vectors",
    )(c_pad, w, b.reshape(depth, 1, n))


def _mod_spec(d, layer, which, tpb, tn=None):
    if tn is None:
        return pl.BlockSpec((None, None, None, 1, d),
                            lambda i, *_: (layer, i // tpb, which, 0, 0))
    return pl.BlockSpec((None, None, None, 1, tn), lambda i, j: (layer, i // tpb, which, 0, j))


def _nm_linear_kernel(x_ref, g_ref, sh_ref, sc_ref, w_ref, *rest, tm, has_tail):
    wt_ref = rest[0] if has_tail else None
    o_ref, h_ref = rest[-2:]
    col_tile = pl.program_id(1)
    last = pl.num_programs(1) - 1

    @pl.when(col_tile == 0)
    def _():
        _normalise_rows(x_ref, g_ref, sh_ref, sc_ref, h_ref, 0, tm)

    def project(wref):
        o_ref[...] = jnp.dot(h_ref[...], wref[...].astype(BF16),
                             preferred_element_type=F32).astype(o_ref.dtype)

    if has_tail:
        pl.when(col_tile < last)(lambda: project(w_ref))
        pl.when(col_tile == last)(lambda: project(wt_ref))
    else:
        project(w_ref)


def _nm_linear(x2, g, mod5, mlayer, sh_idx, w, wlayer, t_len, out_dtype, w_tail=None, tm=1024):
    n, d = x2.shape
    tn = 1024 if (w_tail is None and w.shape[2] % 1024 == 0) else 512
    nmain = w.shape[2] // tn
    ncol = nmain + (0 if w_tail is None else w_tail.shape[1] // tn)
    tm = min(tm, t_len)
    tpb = t_len // tm
    kern = functools.partial(_nm_linear_kernel, tm=tm, has_tail=w_tail is not None)
    in_specs = [pl.BlockSpec((tm, d), lambda i, j: (i, 0)),
                pl.BlockSpec((1, d), lambda i, j: (0, 0)),
                _mod_spec(d, mlayer, sh_idx, tpb),
                _mod_spec(d, mlayer, sh_idx + 1, tpb),
                pl.BlockSpec((None, d, tn), lambda i, j: (wlayer, 0, jnp.minimum(j, nmain - 1)))]
    args = [x2, g.reshape(1, d), mod5, mod5, w]
    if w_tail is not None:
        in_specs.append(pl.BlockSpec((d, tn), lambda i, j: (0, jnp.maximum(j - nmain, 0))))
        args.append(w_tail)
    return pl.pallas_call(
        kern,
        grid=(n // tm, ncol),
        in_specs=in_specs,
        out_specs=pl.BlockSpec((tm, tn), lambda i, j: (i, j)),
        out_shape=jax.ShapeDtypeStruct((n, ncol * tn), out_dtype),
        scratch_shapes=[pltpu.VMEM((tm, d), BF16)],
        compiler_params=_cparams("parallel", "arbitrary"),
        name="norm_mod_linear",
    )(*args)


def _ffn_up_kernel(x_ref, xp_ref, g_ref, sh_ref, sc_ref, wg_ref, wv_ref, cw_ref, cb_ref,
                   o_ref, h_ref, ext_ref, *, tm, rchunk, tpb):
    first = (pl.program_id(0) % tpb) == 0
    col_tile = pl.program_id(1)
    halo = BF16_ROWS

    @pl.when(col_tile == 0)
    def _():
        _normalise_rows(xp_ref, g_ref, sh_ref, sc_ref, h_ref, 0, halo)
        _normalise_rows(x_ref, g_ref, sh_ref, sc_ref, h_ref, halo, tm)

    ext_ref[...] = jnp.dot(h_ref[...], wg_ref[...].astype(BF16), preferred_element_type=F32)
    ext_ref[0:halo, :] = jnp.where(first, 0.0, ext_ref[0:halo, :])
    val = jnp.dot(h_ref[halo:halo + tm, :], wv_ref[...].astype(BF16),
                  preferred_element_type=F32)
    for r0 in range(0, tm, rchunk):
        base = halo + r0
        conv = (cw_ref[0:1, :] * ext_ref[base - 2:base - 2 + rchunk, :]
                + cw_ref[1:2, :] * ext_ref[base - 1:base - 1 + rchunk, :]
                + cw_ref[2:3, :] * ext_ref[base:base + rchunk, :]
                + cb_ref[...])
        o_ref[r0:r0 + rchunk, :] = (_silu(conv) * val[r0:r0 + rchunk, :]).astype(o_ref.dtype)


def _ffn_up(x2, g, mod5, sh_idx, w_up, conv_w, conv_b, layer, t_len, tm=1024, tn=512):
    n, d = x2.shape
    f = w_up.shape[2] // 2
    tm = min(tm, t_len)
    tpb = t_len // tm
    rb = tm // BF16_ROWS
    nf = f // tn
    kern = functools.partial(_ffn_up_kernel, tm=tm, rchunk=min(256, tm), tpb=tpb)
    return pl.pallas_call(
        kern,
        grid=(n // tm, nf),
        in_specs=[pl.BlockSpec((tm, d), lambda i, j: (i, 0)),
                  pl.BlockSpec((BF16_ROWS, d), lambda i, j: (jnp.maximum(i * rb - 1, 0), 0)),
                  pl.BlockSpec((1, d), lambda i, j: (0, 0)),
                  _mod_spec(d, layer, sh_idx, tpb),
                  _mod_spec(d, layer, sh_idx + 1, tpb),
                  pl.BlockSpec((None, d, tn), lambda i, j: (layer, 0, j)),
                  pl.BlockSpec((None, d, tn), lambda i, j: (layer, 0, nf + j)),
                  pl.BlockSpec((None, FFN_CONV, tn), lambda i, j: (layer, 0, j)),
                  pl.BlockSpec((None, 1, tn), lambda i, j: (layer, 0, j))],
        out_specs=pl.BlockSpec((tm, tn), lambda i, j: (i, j)),
        out_shape=jax.ShapeDtypeStruct((n, f), BF16),
        scratch_shapes=[pltpu.VMEM((tm + BF16_ROWS, d), BF16),
                        pltpu.VMEM((tm + BF16_ROWS, tn), F32)],
        compiler_params=_cparams("parallel", "arbitrary"),
        name="ffn_up_act",
    )(x2, x2, g.reshape(1, d), mod5, mod5, w_up, w_up, conv_w,
      conv_b.reshape(conv_b.shape[0], 1, f))


def _linear_res_kernel(a_ref, w_ref, x_ref, gate_ref, o_ref):
    acc = jnp.dot(a_ref[...], w_ref[...], preferred_element_type=F32)
    o_ref[...] = x_ref[...] + gate_ref[...] * acc


def _linear_res(a, w, wlayer, x2, mod5, mlayer, gate_idx, t_len, tm=1024):
    n, k = a.shape
    m = w.shape[2]
    tn = 1024 if k <= 2048 else 512
    tm = min(tm, t_len)
    tpb = t_len // tm
    return pl.pallas_call(
        _linear_res_kernel,
        grid=(n // tm, m // tn),
        in_specs=[pl.BlockSpec((tm, k), lambda i, j: (i, 0)),
                  pl.BlockSpec((None, k, tn), lambda i, j: (wlayer, 0, j)),
                  pl.BlockSpec((tm, tn), lambda i, j: (i, j)),
                  _mod_spec(m, mlayer, gate_idx, tpb, tn)],
        out_specs=pl.BlockSpec((tm, tn), lambda i, j: (i, j)),
        out_shape=jax.ShapeDtypeStruct((n, m), F32),
        compiler_params=_cparams("parallel", "arbitrary"),
        name="linear_residual",
    )(a, w, x2, mod5)


def _retention_kernel(q_ref, k_ref, v_ref, g_ref, cos_ref, sin_ref, dint_ref, dq_ref, dk_ref,
                      dc_ref, o_ref, s_ref, *, heads, dk, dv):
    chunk = pl.program_id(1)

    @pl.when(chunk == 0)
    def _():
        s_ref[...] = jnp.zeros_like(s_ref)

    cos = cos_ref[...]
    sin = sin_ref[...]
    half = dk // 2

    def rot(ref, h):
        x1 = ref[:, h * dk:h * dk + half].astype(F32)
        x2 = ref[:, h * dk + half:(h + 1) * dk].astype(F32)
        return jnp.concatenate([x1 * cos - x2 * sin, x1 * sin + x2 * cos], axis=-1)

    hs = range(heads)
    q = [rot(q_ref, h) for h in hs]
    k = [rot(k_ref, h) * (dk ** -0.5) for h in hs]
    v = [v_ref[:, h * dv:(h + 1) * dv] for h in hs]
    scores = [_mm_nt(q[h], k[h]) * dint_ref[h] for h in hs]
    state = [s_ref[h] for h in hs]
    out = [_mm(scores[h], v[h]) + _mm(q[h] * dq_ref[h], state[h]) for h in hs]
    for h in hs:
        s_ref[h] = dc_ref[h] * state[h] + _mm((k[h] * dk_ref[h]).T, v[h])
    for h in hs:
        oc = out[h] - jnp.mean(out[h], axis=-1, keepdims=True)
        o = oc * lax.rsqrt(jnp.mean(oc * oc, axis=-1, keepdims=True) + EPS)
        gate = g_ref[:, h * dv:(h + 1) * dv].astype(F32)
        o_ref[:, h * dv:(h + 1) * dv] = (o * _silu(gate)).astype(o_ref.dtype)


def _retention_tables(t_len, dk):
    half = dk // 2
    inv = ROPE_BASE ** (-np.arange(half, dtype=np.float64) / half)
    ang = np.arange(t_len, dtype=np.float64)[:, None] * inv[None, :]
    lg = np.log1p(-np.power(2.0, -5.0 - np.arange(RET_HEADS, dtype=np.float64)))
    idx = np.arange(RET_CHUNK, dtype=np.float64)
    diff = idx[:, None] - idx[None, :]
    dint = np.where(diff >= 0, np.exp(lg[:, None, None] * np.maximum(diff, 0.0)), 0.0)
    dq = np.exp(lg[:, None] * (idx + 1.0))[..., None]
    dkk = np.exp(lg[:, None] * (RET_CHUNK - 1.0 - idx))[..., None]
    dc = np.exp(lg * RET_CHUNK)[:, None, None]
    f = lambda a: jnp.asarray(a, F32)
    return f(np.cos(ang)), f(np.sin(ang)), f(dint), f(dq), f(dkk), f(dc)


def _retention(qkvg, n_b, t_len, d):
    n = qkvg.shape[0]
    heads = RET_HEADS
    dk = d // heads
    dv = 2 * dk
    lc = RET_CHUNK
    nc = t_len // lc
    cos, sin, dint, dq, dkk, dc = _retention_tables(t_len, dk)
    row = lambda b, c: b * nc + c
    full = lambda a: pl.BlockSpec(a.shape, lambda b, c: (0,) * a.ndim)
    kern = functools.partial(_retention_kernel, heads=heads, dk=dk, dv=dv)
    return pl.pallas_call(
        kern,
        grid=(n_b, nc),
        in_specs=[pl.BlockSpec((lc, d), lambda b, c: (row(b, c), 0)),
                  pl.BlockSpec((lc, d), lambda b, c: (row(b, c), 1)),
                  pl.BlockSpec((lc, 2 * d), lambda b, c: (row(b, c), 1)),
                  pl.BlockSpec((lc, 2 * d), lambda b, c: (row(b, c), 2)),
                  pl.BlockSpec((lc, dk // 2), lambda b, c: (c, 0)),
                  pl.BlockSpec((lc, dk // 2), lambda b, c: (c, 0)),
                  full(dint), full(dq), full(dkk), full(dc)],
        out_specs=pl.BlockSpec((lc, 2 * d), lambda b, c: (row(b, c), 0)),
        out_shape=jax.ShapeDtypeStruct((n, 2 * d), BF16),
        scratch_shapes=[pltpu.VMEM((heads, dk, dv), F32)],
        compiler_params=_cparams("parallel", "arbitrary"),
        name="retention_chunk",
    )(qkvg, qkvg, qkvg, qkvg, cos, sin, dint, dq, dkk, dc)


def _mlstm_kernel(q_ref, k_ref, v_ref, og_ref, gt_ref, cw_ref, cb_ref, gb_ref, ng_ref, tril_ref,
                  o_ref, qext, kext, c_ref, n_ref, m_ref, *, lc, dqk, dv, heads):
    chunk = pl.program_id(1)
    qw = heads * dqk

    @pl.when(chunk == 0)
    def _():
        qext[0:SUBLANES, :] = jnp.zeros((SUBLANES, qw), F32)
        kext[0:SUBLANES, :] = jnp.zeros((SUBLANES, qw), F32)
        c_ref[...] = jnp.zeros_like(c_ref)
        n_ref[...] = jnp.zeros_like(n_ref)
        m_ref[...] = jnp.zeros_like(m_ref)

    def conv_silu(raw_ref, ext, c0):
        ext[SUBLANES:SUBLANES + lc, :] = raw_ref[...]
        y = cb_ref[:, c0:c0 + qw]
        for j in range(MLSTM_CONV):
            off = SUBLANES - (MLSTM_CONV - 1) + j
            y = y + cw_ref[j:j + 1, c0:c0 + qw] * ext[off:off + lc, :]
        ext[0:SUBLANES, :] = ext[lc:lc + SUBLANES, :]
        return _silu(y)

    q_all = conv_silu(q_ref, qext, 0)
    k_all = conv_silu(k_ref, kext, qw) * (dqk ** -0.5)

    gt = gt_ref[...] + gb_ref[...]
    lane = lax.broadcasted_iota(jnp.int32, gt.shape, 1)
    tril = tril_ref[...]
    causal = (lax.broadcasted_iota(jnp.int32, (lc, lc), 0)
              >= lax.broadcasted_iota(jnp.int32, (lc, lc), 1))

    hs = range(heads)
    q = [q_all[:, h * dqk:(h + 1) * dqk] for h in hs]
    k = [k_all[:, h * dqk:(h + 1) * dqk] for h in hs]
    v = [v_ref[:, h * dv:(h + 1) * dv] for h in hs]
    ipre = [jnp.sum(jnp.where(lane == h, gt, 0.0), axis=-1, keepdims=True) for h in hs]
    fpre = [jnp.sum(jnp.where(lane == h + heads, gt, 0.0), axis=-1, keepdims=True) for h in hs]
    li = [GATE_CAP * jnp.tanh(x / GATE_CAP) for x in ipre]
    lf = [-_softplus(-(GATE_CAP * jnp.tanh(x / GATE_CAP))) for x in fpre]
    b_full = [_mm_split_rhs(tril, jnp.broadcast_to(x, (lc, lc))) for x in lf]
    b_col = [x[:, 0:1] for x in b_full]
    dmat = [jnp.where(causal, b_full[h] - b_full[h].T + jnp.broadcast_to(li[h], (lc, lc)).T,
                      NEG_BIG) for h in hs]
    m_st = [m_ref[h] for h in hs]
    m_inter = [b_col[h] + m_st[h] for h in hs]
    m_t = [jnp.maximum(m_inter[h], jnp.max(dmat[h], axis=-1, keepdims=True)) for h in hs]
    scores = [_mm_nt(q[h], k[h]) * jnp.exp(dmat[h] - m_t[h]) for h in hs]
    w_inter = [jnp.exp(m_inter[h] - m_t[h]) for h in hs]
    c_st = [c_ref[h] for h in hs]
    n_st = [n_ref[h] for h in hs]
    num = [_mm(scores[h], v[h]) + w_inter[h] * _mm(q[h], c_st[h]) for h in hs]
    den = [jnp.sum(scores[h], axis=-1, keepdims=True)
           + w_inter[h] * jnp.sum(q[h] * n_st[h], axis=-1, keepdims=True) for h in hs]
    for h in hs:
        b_last = b_col[h][lc - 1:lc, :]
        gdec = b_last - b_col[h] + li[h]
        m_new = jnp.maximum(b_last + m_st[h], jnp.max(gdec, axis=0, keepdims=True))
        wk = jnp.exp(gdec - m_new)
        carry = jnp.exp(b_last + m_st[h] - m_new)
        kw = k[h] * wk
        c_ref[h] = carry * c_st[h] + _mm(kw.T, v[h])
        n_ref[h] = carry * n_st[h] + jnp.sum(kw, axis=0, keepdims=True)
        m_ref[h] = m_new
    for h in hs:
        hh = num[h] / jnp.maximum(jnp.abs(den[h]), jnp.exp(-m_t[h]))
        cs = slice(h * dv, (h + 1) * dv)
        hn = hh * lax.rsqrt(jnp.mean(hh * hh, axis=-1, keepdims=True) + EPS) * ng_ref[:, cs]
        o_ref[:, cs] = (hn * jax.nn.sigmoid(og_ref[:, cs])).astype(o_ref.dtype)


def _mlstm(proj, conv_w, conv_b, gate_b, norm_g, n_b, t_len, d):
    n = proj.shape[0]
    heads = MLSTM_HEADS
    dqk = d // (2 * heads)
    dv = d // heads
    lc = MLSTM_CHUNK
    nc = t_len // lc
    qw = heads * dqk
    gtb = (2 * qw + 2 * d) // LANES
    gb = jnp.zeros((1, LANES), F32).at[0, :2 * heads].set(gate_b.astype(F32))
    tril = jnp.asarray(np.tril(np.ones((lc, lc), np.float32)), BF16)
    kern = functools.partial(_mlstm_kernel, lc=lc, dqk=dqk, dv=dv, heads=heads)
    row = lambda b, c: b * nc + c
    full = lambda a: pl.BlockSpec(a.shape, lambda b, c: (0,) * a.ndim)
    cb = conv_b.reshape(1, 2 * qw)
    ng = norm_g.reshape(1, d)
    return pl.pallas_call(
        kern,
        grid=(n_b, nc),
        in_specs=[pl.BlockSpec((lc, qw), lambda b, c: (row(b, c), 0)),
                  pl.BlockSpec((lc, qw), lambda b, c: (row(b, c), 1)),
                  pl.BlockSpec((lc, d), lambda b, c: (row(b, c), (2 * qw) // d)),
                  pl.BlockSpec((lc, d), lambda b, c: (row(b, c), (2 * qw) // d + 1)),
                  pl.BlockSpec((lc, LANES), lambda b, c: (row(b, c), gtb)),
                  full(conv_w), full(cb), full(gb), full(ng), full(tril)],
        out_specs=pl.BlockSpec((lc, d), lambda b, c: (row(b, c), 0)),
        out_shape=jax.ShapeDtypeStruct((n, d), BF16),
        scratch_shapes=[pltpu.VMEM((lc + SUBLANES, qw), F32),
                        pltpu.VMEM((lc + SUBLANES, qw), F32),
                        pltpu.VMEM((heads, dqk, dv), F32),
                        pltpu.VMEM((heads, 1, dqk), F32),
                        pltpu.VMEM((heads, 1, 1), F32)],
        compiler_params=_cparams("parallel", "arbitrary"),
        name="mlstm_chunk",
    )(proj, proj, proj, proj, proj, conv_w, cb, gb, ng, tril)


def _rwkv_kernel(pr_ref, pk_ref, pv_ref, pl_ref, mur_ref, muk_ref, muv_ref, mul_ref,
                 w0_ref, a0_ref, kk_ref, ka_ref, lnw_ref, lnb_ref, rk_ref,
                 w2_ref, a2_ref, g2_ref, tril_ref, ms_ref, mi_ref, eye_ref,
                 o_ref, h_ref, prev_ref, ext_ref, y_ref, *, lc, nchunk, npp):
    block = pl.program_id(2)
    lb = lc * nchunk
    pw = npp * LANES

    @pl.when(block == 0)
    def _():
        h_ref[...] = jnp.zeros_like(h_ref)
        prev_ref[...] = jnp.zeros_like(prev_ref)

    lane = lax.broadcasted_iota(jnp.int32, (lb, LANES), 1)
    head0 = lane < RWKV_HEAD
    head0_c = lax.broadcasted_iota(jnp.int32, (lc, LANES), 1) < RWKV_HEAD

    def pair_sum(x):
        s0 = jnp.sum(jnp.where(head0, x, 0.0), axis=-1, keepdims=True)
        s1 = jnp.sum(jnp.where(head0, 0.0, x), axis=-1, keepdims=True)
        return jnp.where(head0, s0, s1)

    def shifted(src_ref, mu_ref, c0, pc0):
        cs = slice(c0, c0 + LANES)
        ext_ref[0:SUBLANES, :] = prev_ref[:, pc0:pc0 + LANES]
        ext_ref[SUBLANES:SUBLANES + lb, :] = src_ref[:, cs]
        cur = src_ref[:, cs]
        return cur + mu_ref[:, cs] * (ext_ref[SUBLANES - 1:SUBLANES - 1 + lb, :] - cur)

    lo0 = 3 * pw
    wlo = jnp.tanh(shifted(pl_ref, mul_ref, 0, lo0)).astype(BF16)
    alo = shifted(pl_ref, mul_ref, LANES, lo0 + LANES).astype(BF16)
    glo = jax.nn.sigmoid(jnp.concatenate(
        [shifted(pl_ref, mul_ref, 2 * LANES + s, lo0 + 2 * LANES + s)
         for s in range(0, RWKV_GATE_LORA, LANES)], axis=-1)).astype(BF16)

    r_p, k_p, v_p, kn_p, bb_p, wl_p, g_p = [], [], [], [], [], [], []
    for pi in range(npp):
        cs = slice(pi * LANES, (pi + 1) * LANES)
        r = shifted(pr_ref, mur_ref, pi * LANES, pi * LANES)
        k = shifted(pk_ref, muk_ref, pi * LANES, pw + pi * LANES)
        v = shifted(pv_ref, muv_ref, pi * LANES, 2 * pw + pi * LANES)
        wraw = w0_ref[:, cs] + jnp.dot(wlo, w2_ref[:, cs], preferred_element_type=F32)
        ag = jax.nn.sigmoid(a0_ref[:, cs] + jnp.dot(alo, a2_ref[:, cs],
                                                    preferred_element_type=F32))
        kn = k * kk_ref[:, cs]
        kn = kn / jnp.maximum(jnp.sqrt(pair_sum(kn * kn)), 1e-12)
        r_p.append(r)
        k_p.append(k * (1.0 + (ag - 1.0) * ka_ref[:, cs]))
        v_p.append(v)
        kn_p.append(kn)
        bb_p.append(kn * ag)
        wl_p.append(-jnp.exp(-_softplus(-wraw) - 0.5))
        g_p.append(jnp.dot(glo, g2_ref[:, cs], preferred_element_type=F32))

    tail = slice(lb - SUBLANES, lb)
    prev_ref[:, 0:pw] = pr_ref[tail, :]
    prev_ref[:, pw:2 * pw] = pk_ref[tail, :]
    prev_ref[:, 2 * pw:3 * pw] = pv_ref[tail, :]
    prev_ref[:, 3 * pw:3 * pw + RWKV_LORA_W] = pl_ref[tail, :]

    def stack(x):
        return jnp.concatenate([jnp.where(head0_c, x, 0.0), jnp.where(head0_c, 0.0, x)], axis=0)

    tril = tril_ref[...]
    ms = ms_ref[...]
    mi = mi_ref[...]
    eye = eye_ref[...]
    two = 2 * lc
    probs = [(pi, ci) for ci in range(nchunk) for pi in range(npp)]
    ps = range(len(probs))

    def rows(vals):
        return [vals[pi][ci * lc:(ci + 1) * lc, :] for pi, ci in probs]

    wl = rows(wl_p)
    cum = [_mm_split_rhs(tril, x) for x in wl]
    cl = [x[lc - 1:lc, :] for x in cum]
    r, k, v, kn, bb = rows(r_p), rows(k_p), rows(v_p), rows(kn_p), rows(bb_p)
    e_n = [jnp.exp(-cum[i]) for i in ps]
    e_l = [jnp.exp(cl[i] - cum[i]) for i in ps]
    a_s = [stack(-kn[i] * jnp.exp(cum[i] - wl[i])) for i in ps]
    r_s = [stack(r[i] * jnp.exp(cum[i])) for i in ps]
    b_s = [stack(bb[i] * e_n[i]) for i in ps]
    k_s = [stack(k[i] * e_n[i]) for i in ps]
    v_s = [stack(v[i]) for i in ps]
    bp_t = [stack(bb[i] * e_l[i]).T for i in ps]
    kp_t = [stack(k[i] * e_l[i]).T for i in ps]

    gm = [_mm_nt(jnp.concatenate([a_s[i], r_s[i]], axis=0),
                 jnp.concatenate([b_s[i], k_s[i]], axis=0)) for i in ps]
    npow = [gm[i][:two, :two] * ms for i in ps]
    aak_v = [_mm(gm[i][:two, two:] * ms, v_s[i]) for i in ps]
    arb = [gm[i][two:, :two] * mi for i in ps]
    ark_v = [_mm(gm[i][two:, two:] * mi, v_s[i]) for i in ps]
    kp_v = [_mm(kp_t[i], v_s[i]) for i in ps]

    tinv = [eye + npow[i] for i in ps]
    for _ in range(int(np.log2(lc)) - 1):
        npow = [_mm(npow[i], npow[i]) for i in ps]
        tinv = [_mm(tinv[i], eye + npow[i]) for i in ps]

    taw = [_mm(tinv[i], jnp.concatenate([a_s[i], aak_v[i]], axis=1)) for i in ps]
    arb_taw = [_mm(arb[i], taw[i]) for i in ps]
    bp_taw = [_mm(bp_t[i], taw[i]) for i in ps]
    qh = [r_s[i] + arb_taw[i][:, :LANES] for i in ps]
    yin = [arb_taw[i][:, LANES:] + ark_v[i] for i in ps]
    phi = [eye * jnp.exp(cl[i]) + bp_taw[i][:, :LANES] for i in ps]
    psi = [bp_taw[i][:, LANES:] + kp_v[i] for i in ps]

    hst = [h_ref[pi] for pi in range(npp)]
    for i, (pi, ci) in enumerate(probs):
        ys = _mm(qh[i], hst[pi]) + yin[i]
        hst[pi] = _mm(phi[i], hst[pi]) + psi[i]
        y_ref[pi, ci * lc:(ci + 1) * lc, :] = ys[:lc] + ys[lc:]
    for pi in range(npp):
        h_ref[pi] = hst[pi]

    inv_head = 1.0 / RWKV_HEAD
    for pi in range(npp):
        cs = slice(pi * LANES, (pi + 1) * LANES)
        y = y_ref[pi]
        yc = y - pair_sum(y) * inv_head
        var = pair_sum(yc * yc) * inv_head
        yn = yc * lax.rsqrt(var + RWKV_LN_EPS) * lnw_ref[:, cs] + lnb_ref[:, cs]
        bonus = pair_sum(r_p[pi] * k_p[pi] * rk_ref[:, cs])
        o_ref[:, cs] = ((yn + bonus * v_p[pi]) * g_p[pi]).astype(o_ref.dtype)


def _rwkv_mix(p, mu, w0, a0, k_k, k_a, ln_w, ln_b, r_k, w2, a2, g2, n_b, t_len, d,
              nchunk=4, npp=4):
    lc = RWKV_CHUNK
    lb = lc * nchunk
    nb = t_len // lb
    pw = npp * LANES
    ngrp = d // pw
    two = 2 * lc
    lora_blk = (3 * d) // RWKV_LORA_W
    idx = np.arange(two)
    same = (idx[:, None] // lc) == (idx[None, :] // lc)
    ms = jnp.asarray((same & (idx[:, None] > idx[None, :])).astype(np.float32))
    mi = jnp.asarray((same & (idx[:, None] >= idx[None, :])).astype(np.float32))
    eye = jnp.asarray(np.eye(two, dtype=np.float32))
    tril = jnp.asarray(np.tril(np.ones((lc, lc), np.float32)), BF16)
    row = lambda b, t: b * nb + t
    pspec = lambda off: pl.BlockSpec((lb, pw), lambda b, g, t: (row(b, t), off + g))
    mspec = lambda off: pl.BlockSpec((1, pw), lambda b, g, t: (0, off + g))
    vspec = pl.BlockSpec((1, pw), lambda b, g, t: (0, g))
    wspec = lambda rows: pl.BlockSpec((rows, pw), lambda b, g, t: (0, g))
    sq = lambda s: pl.BlockSpec((s, s), lambda b, g, t: (0, 0))
    vec = lambda a: a.reshape(1, -1)
    mu2 = vec(mu)
    kern = functools.partial(_rwkv_kernel, lc=lc, nchunk=nchunk, npp=npp)
    return pl.pallas_call(
        kern,
        grid=(n_b, ngrp, nb),
        in_specs=[pspec(0), pspec(ngrp), pspec(2 * ngrp),
                  pl.BlockSpec((lb, RWKV_LORA_W), lambda b, g, t: (row(b, t), lora_blk)),
                  mspec(0), mspec(ngrp), mspec(2 * ngrp),
                  pl.BlockSpec((1, RWKV_LORA_W), lambda b, g, t: (0, lora_blk)),
                  vspec, vspec, vspec, vspec, vspec, vspec, vspec,
                  wspec(LANES), wspec(LANES), wspec(RWKV_GATE_LORA),
                  sq(lc), sq(two), sq(two), sq(two)],
        out_specs=pl.BlockSpec((lb, pw), lambda b, g, t: (row(b, t), g)),
        out_shape=jax.ShapeDtypeStruct((n_b * t_len, d), BF16),
        scratch_shapes=[pltpu.VMEM((npp, LANES, LANES), F32),
                        pltpu.VMEM((SUBLANES, 3 * pw + RWKV_LORA_W), F32),
                        pltpu.VMEM((lb + SUBLANES, LANES), F32),
                        pltpu.VMEM((npp, lb, LANES), F32)],
        compiler_params=_cparams("parallel", "parallel", "arbitrary"),
        name="rwkv_mix",
    )(p, p, p, p, mu2, mu2, mu2, mu2, vec(w0), vec(a0), vec(k_k), vec(k_a), vec(ln_w),
      vec(ln_b), vec(r_k), w2, a2, g2, tril, ms, mi, eye)


def _final_kernel(x_ref, g_ref, sh_ref, sc_ref, o_ref):
    o_ref[...] = _rms_mod(x_ref[...], g_ref[...], sh_ref[...], sc_ref[...])


def _final_norm(x2, g, mod5, t_len, tm=256):
    n, d = x2.shape
    tm = min(tm, t_len)
    tpb = t_len // tm
    return pl.pallas_call(
        _final_kernel,
        grid=(n // tm,),
        in_specs=[pl.BlockSpec((tm, d), lambda i: (i, 0)),
                  pl.BlockSpec((1, d), lambda i: (0, 0)),
                  _mod_spec(d, 0, 0, tpb),
                  _mod_spec(d, 0, 1, tpb)],
        out_specs=pl.BlockSpec((tm, d), lambda i: (i, 0)),
        out_shape=jax.ShapeDtypeStruct((n, d), F32),
        compiler_params=_cparams("parallel"),
        name="final_norm",
    )(x2, g.reshape(1, d), mod5, mod5)


def _pad_cols(w, width):
    return jnp.pad(w, ((0, 0), (0, width - w.shape[1])))


def _rwkv_pack_lora(lora_cols, mu, d):
    c1 = RWKV_DECAY_LORA
    c2 = c1 + RWKV_A_LORA
    def pack(a):
        return jnp.concatenate([_pad_cols(a[:, :c1], LANES), _pad_cols(a[:, c1:c2], LANES),
                                a[:, c2:]], axis=-1)
    mu2 = mu.reshape(1, -1)
    return pack(lora_cols), jnp.concatenate([mu2[:, :3 * d], pack(mu2[:, 3 * d:])], axis=-1)[0]


def _pad_rows(w, rows):
    return jnp.pad(w, ((0, rows - w.shape[0]), (0, 0)))


def kernel(x, c, mod_w, mod_b, norm_mix_g, norm_ffn_g, ret_w_in, ret_w_out, rwkv_w_in, rwkv_mu, rwkv_w0, rwkv_w2, rwkv_a0, rwkv_a2, rwkv_g2, rwkv_k_k, rwkv_k_a, rwkv_r_k, rwkv_ln_w, rwkv_ln_b, rwkv_w_out, mlstm_w_in, mlstm_conv_w, mlstm_conv_b, mlstm_gate_b, mlstm_norm_g, mlstm_w_out, ffn_w_up, ffn_conv_w, ffn_conv_b, ffn_w_down, final_g, final_mod_w, final_mod_b):
    n_b, t_len, d = x.shape
    depth = mod_w.shape[0]
    x2 = x.reshape(n_b * t_len, d)

    c_pad = jnp.pad(c, ((0, SUBLANES - n_b), (0, 0)))
    mod5 = _mod_vectors(c_pad, mod_w, mod_b).reshape(depth, SUBLANES, 6, 1, d)
    fmod5 = _mod_vectors(c_pad, final_mod_w[None], final_mod_b[None]).reshape(1, SUBLANES, 2, 1, d)
    mix, ffn = 0, 3

    ret_w_out16 = ret_w_out.astype(BF16)
    rwkv_w_out16 = rwkv_w_out.astype(BF16)
    mlstm_w_out16 = mlstm_w_out.astype(BF16)
    ffn_w_down16 = ffn_w_down.astype(BF16)
    rwkv_w_in16 = rwkv_w_in.astype(BF16)
    mlstm_w_in16 = mlstm_w_in.astype(BF16)

    for i in range(depth):
        kind, j = i % N_MIXERS, i // N_MIXERS
        if kind == 0:
            qkvg = _nm_linear(x2, norm_mix_g[i], mod5, i, mix, ret_w_in, j, t_len, BF16)
            o = _retention(qkvg, n_b, t_len, d)
            x2 = _linear_res(o, ret_w_out16, j, x2, mod5, i, mix + 2, t_len)
        elif kind == 1:
            lora_w, mu = _rwkv_pack_lora(rwkv_w_in16[j, :, 3 * d:], rwkv_mu[j], d)
            p = _nm_linear(x2, norm_mix_g[i], mod5, i, mix, rwkv_w_in16, j, t_len, F32,
                           w_tail=lora_w)
            o = _rwkv_mix(p, mu, rwkv_w0[j], rwkv_a0[j], rwkv_k_k[j], rwkv_k_a[j], rwkv_ln_w[j],
                          rwkv_ln_b[j], rwkv_r_k[j],
                          _pad_rows(rwkv_w2[j], LANES).astype(BF16),
                          _pad_rows(rwkv_a2[j], LANES).astype(BF16),
                          rwkv_g2[j].astype(BF16), n_b, t_len, d)
            x2 = _linear_res(o, rwkv_w_out16, j, x2, mod5, i, mix + 2, t_len)
        else:
            gate_w = _pad_cols(mlstm_w_in16[j, :, 3 * d:], 4 * LANES)
            proj = _nm_linear(x2, norm_mix_g[i], mod5, i, mix, mlstm_w_in16, j, t_len, F32,
                              w_tail=gate_w)
            o = _mlstm(proj, mlstm_conv_w[j], mlstm_conv_b[j], mlstm_gate_b[j], mlstm_norm_g[j],
                       n_b, t_len, d)
            x2 = _linear_res(o, mlstm_w_out16, j, x2, mod5, i, mix + 2, t_len)
        act = _ffn_up(x2, norm_ffn_g[i], mod5, ffn, ffn_w_up, ffn_conv_w, ffn_conv_b, i, t_len)
        x2 = _linear_res(act, ffn_w_down16, i, x2, mod5, i, ffn + 2, t_len)

    out = _final_norm(x2, final_g, fmod5, t_len)
    return out.reshape(n_b, t_len, d)
```
